```python
import math
import jax, jax.numpy as jnp
from jax import lax
import numpy as np

D_MODEL = 1024
BATCH = 4
SEQ = 4096
DEPTH = 2

N_MIXERS = 2
N_SSM_LAYERS = (DEPTH + 1) // 2
N_ATTN_LAYERS = DEPTH // 2
N_SUB = 3
D_FF = 2816
SSM_EXPAND = 2
D_INNER = SSM_EXPAND * D_MODEL
SSM_HEADDIM = 64
SSM_HEADS = D_INNER // SSM_HEADDIM
SSM_GROUPS = 4
SSM_STATE = 128
CONV_WIDTH = 3
CHUNK = 128
CONV_CH = D_INNER + 2 * SSM_GROUPS * SSM_STATE
SSM_IN = D_INNER + CONV_CH + 2 * SSM_HEADS
DA_HEADS = 8
DA_HEAD_DIM = 64
DA_V_DIM = 2 * DA_HEAD_DIM
DA_QK = DA_HEADS * 2 * DA_HEAD_DIM
DA_QKV = 2 * DA_QK + DA_HEADS * DA_V_DIM
Q_BLOCK = 128
ALPHA = (2 * DEPTH) ** 0.25
BETA = (8 * DEPTH) ** -0.25
LN_EPS = 1e-5

kernel_name = "hybrid_ssd_diffattn_macaron_deepnorm_adaln"


def layer_norm(x, g, b):
    xf = x.astype(jnp.float32)
    mu = jnp.mean(xf, axis=-1, keepdims=True)
    var = jnp.mean(jnp.square(xf - mu), axis=-1, keepdims=True)
    y = (xf - mu) * lax.rsqrt(var + LN_EPS) * g.astype(jnp.float32) + b.astype(jnp.float32)
    return y.astype(x.dtype)


def rms_norm(x, g):
    xf = x.astype(jnp.float32)
    y = xf * lax.rsqrt(jnp.mean(jnp.square(xf), axis=-1, keepdims=True) + LN_EPS)
    return (y * g.astype(jnp.float32)).astype(x.dtype)


def modulate(x, m):
    return x * (1 + m[:, 1][:, None, :]) + m[:, 0][:, None, :]


def swiglu(u, w_gate, w_up, w_down):
    return (jax.nn.silu(u @ w_gate) * (u @ w_up)) @ w_down


def depthwise_conv(x, w, bias):
    ch = x.shape[-1]
    y = lax.conv_general_dilated(
        x, w[:, None, :].astype(x.dtype), window_strides=(1,),
        padding=[(CONV_WIDTH // 2, CONV_WIDTH // 2)],
        dimension_numbers=('NWC', 'WIO', 'NWC'), feature_group_count=ch)
    return y + bias.astype(x.dtype)


def segsum(a):
    t = a.shape[-1]
    cs = jnp.cumsum(a, axis=-1)
    seg = cs[..., :, None] - cs[..., None, :]
    mask = jnp.tril(jnp.ones((t, t), dtype=bool))
    return jnp.where(mask, seg, -jnp.inf)


def ssd_chunked(X, dt, A, Bm, Cm):
    b, t, h, p = X.shape
    g, n = Bm.shape[-2:]
    j = h // g
    c = t // CHUNK
    Xc = (X * dt[..., None]).reshape(b, c, CHUNK, g, j, p)
    a = (dt * A).reshape(b, c, CHUNK, g, j).transpose(0, 3, 4, 1, 2)
    Bc = Bm.reshape(b, c, CHUNK, g, n)
    Cc = Cm.reshape(b, c, CHUNK, g, n)
    a_cs = jnp.cumsum(a, axis=-1)
    L = jnp.exp(segsum(a))
    CB = jnp.einsum('bclgn,bcsgn->bgcls', Cc, Bc)
    y_diag = jnp.einsum('bgcls,bgjcls,bcsgjp->bclgjp', CB, L, Xc)
    decay_states = jnp.exp(a_cs[..., -1:] - a_cs)
    states = jnp.einsum('bclgn,bgjcl,bclgjp->bcgjpn', Bc, decay_states, Xc)
    states = jnp.concatenate([jnp.zeros_like(states[:, :1]), states], axis=1)
    chunk_decay = jnp.exp(segsum(jnp.pad(a_cs[..., -1], ((0, 0), (0, 0), (0, 0), (1, 0)))))
    states = jnp.einsum('bgjzc,bcgjpn->bzgjpn', chunk_decay, states)[:, :-1]
    y_off = jnp.einsum('bclgn,bcgjpn,bgjcl->bclgjp', Cc, states, jnp.exp(a_cs))
    return (y_diag + y_off).reshape(b, t, h, p)


def mamba2_bidir(u, w_in, conv_w, conv_b, dt_bias, a_log, d_skip, norm_g, w_out):
    b, t, _ = u.shape
    f32 = jnp.float32
    proj = u @ w_in
    z, xbc, dt_raw = jnp.split(proj, [D_INNER, D_INNER + CONV_CH], axis=-1)
    xbc = jax.nn.silu(depthwise_conv(xbc, conv_w, conv_b))
    xs, Bm, Cm = jnp.split(xbc, [D_INNER, D_INNER + SSM_GROUPS * SSM_STATE], axis=-1)
    X = xs.reshape(b, t, SSM_HEADS, SSM_HEADDIM).astype(f32)
    Bm = Bm.reshape(b, t, SSM_GROUPS, SSM_STATE).astype(f32)
    Cm = Cm.reshape(b, t, SSM_GROUPS, SSM_STATE).astype(f32)
    dt = jax.nn.softplus(dt_raw.astype(f32).reshape(b, t, 2, SSM_HEADS) + dt_bias.astype(f32))
    A = -jnp.exp(a_log.astype(f32))
    flip = lambda v: jnp.flip(v, axis=1)
    y_f = ssd_chunked(X, dt[:, :, 0], A[0], Bm, Cm)
    y_b = flip(ssd_chunked(flip(X), flip(dt[:, :, 1]), A[1], flip(Bm), flip(Cm)))
    y = y_f + y_b + X * d_skip.astype(f32)[:, None]
    y = y.reshape(b, t, D_INNER) * jax.nn.silu(z.astype(f32))
    yg = y.reshape(b, t, SSM_GROUPS, D_INNER // SSM_GROUPS)
    yg = yg * lax.rsqrt(jnp.mean(jnp.square(yg), axis=-1, keepdims=True) + LN_EPS)
    y = yg.reshape(b, t, D_INNER) * norm_g.astype(f32)
    return y.astype(u.dtype) @ w_out


def alibi_slopes(n_heads):
    return jnp.asarray(np.array([2.0 ** (-8.0 * (h + 1) / n_heads) for h in range(n_heads)], dtype=np.float32))


def lambda_init_fn(layer_idx):
    return 0.8 - 0.6 * math.exp(-0.3 * layer_idx)


def diff_attention(u, w_qkv, lam, subln_g, w_out, lambda_init):
    b, t, _ = u.shape
    f32 = jnp.float32
    qkv = u @ w_qkv
    q, k, v = jnp.split(qkv, [DA_QK, 2 * DA_QK], axis=-1)
    q = q.reshape(b, t, DA_HEADS, 2, DA_HEAD_DIM)
    k = k.reshape(b, t, DA_HEADS, 2, DA_HEAD_DIM)
    v = v.reshape(b, t, DA_HEADS, DA_V_DIM)
    lamf = lam.astype(f32)
    lam_full = jnp.exp(jnp.sum(lamf[0] * lamf[1])) - jnp.exp(jnp.sum(lamf[2] * lamf[3])) + lambda_init
    slopes = alibi_slopes(DA_HEADS)
    scale = DA_HEAD_DIM ** -0.5
    n_blk = t // Q_BLOCK
    qb = q.reshape(b, n_blk, Q_BLOCK, DA_HEADS, 2, DA_HEAD_DIM).transpose(1, 0, 2, 3, 4, 5)
    pos_k = jnp.arange(t, dtype=f32)

    def block(args):
        q_blk, i = args
        s = jnp.einsum('bqhmd,bkhmd->bhmqk', q_blk, k).astype(f32) * scale
        pos_q = (i * Q_BLOCK + jnp.arange(Q_BLOCK)).astype(f32)
        dist = jnp.abs(pos_q[:, None] - pos_k[None, :])
        s = s - slopes[None, :, None, None, None] * dist
        p = jax.nn.softmax(s, axis=-1)
        w = p[:, :, 0] - lam_full * p[:, :, 1]
        return jnp.einsum('bhqk,bkhe->bqhe', w.astype(v.dtype), v)

    o = lax.map(block, (qb, jnp.arange(n_blk)))
    o = o.transpose(1, 0, 2, 3, 4).reshape(b, t, DA_HEADS, DA_V_DIM)
    o = rms_norm(o, subln_g) * (1 - lambda_init)
    return o.reshape(b, t, DA_HEADS * DA_V_DIM) @ w_out


def setup_inputs(seed: int = 0) -> dict:
    key = jax.random.key(seed)
    ks = jax.random.split(key, 24)
    f32 = jnp.float32
    nrm = lambda k, s, sc: jax.random.normal(k, s, f32) * sc
    NS, NA, H = N_SSM_LAYERS, N_ATTN_LAYERS, SSM_HEADS
    dt0 = jnp.exp(jax.random.uniform(ks[12], (NS, 2, H), f32) * (math.log(0.1) - math.log(0.001)) + math.log(0.001))
    return {
        "x": nrm(ks[0], (BATCH, SEQ, D_MODEL), 1.0),
        "c": nrm(ks[1], (BATCH, D_MODEL), 1.0),
        "ada_w": nrm(ks[2], (DEPTH, D_MODEL, N_SUB * 3 * D_MODEL), 0.1 * D_MODEL ** -0.5),
        "ada_b": nrm(ks[3], (DEPTH, N_SUB * 3 * D_MODEL), 0.01),
        "ln_g": 1.0 + nrm(ks[4], (DEPTH, N_SUB, D_MODEL), 0.01),
        "ln_b": nrm(ks[5], (DEPTH, N_SUB, D_MODEL), 0.01),
        "ffn_w_gate": nrm(ks[6], (DEPTH, 2, D_MODEL, D_FF), D_MODEL ** -0.5),
        "ffn_w_up": nrm(ks[7], (DEPTH, 2, D_MODEL, D_FF), D_MODEL ** -0.5),
        "ffn_w_down": nrm(ks[8], (DEPTH, 2, D_FF, D_MODEL), BETA * D_FF ** -0.5),
        "ssm_w_in": nrm(ks[9], (NS, D_MODEL, SSM_IN), D_MODEL ** -0.5),
        "ssm_conv_w": nrm(ks[10], (NS, CONV_WIDTH, CONV_CH), CONV_WIDTH ** -0.5),
        "ssm_conv_b": nrm(ks[11], (NS, CONV_CH), 0.01),
        "ssm_dt_bias": dt0 + jnp.log(-jnp.expm1(-dt0)),
        "ssm_a_log": jnp.log(jax.random.uniform(ks[13], (NS, 2, H), f32, 1.0, 16.0)),
        "ssm_d": 1.0 + nrm(ks[14], (NS, H), 0.01),
        "ssm_norm_g": 1.0 + nrm(ks[15], (NS, D_INNER), 0.01),
        "ssm_w_out": nrm(ks[16], (NS, D_INNER, D_MODEL), BETA * D_INNER ** -0.5),
        "attn_w_qkv": nrm(ks[17], (NA, D_MODEL, DA_QKV), D_MODEL ** -0.5),
        "attn_lambda": nrm(ks[18], (NA, 4, DA_HEAD_DIM), 0.1),
        "attn_subln_g": 1.0 + nrm(ks[19], (NA, DA_V_DIM), 0.01),
        "attn_w_out": nrm(ks[20], (NA, DA_HEADS * DA_V_DIM, D_MODEL), BETA * (DA_HEADS * DA_V_DIM) ** -0.5),
    }


def reference(x, c, ada_w, ada_b, ln_g, ln_b, ffn_w_gate, ffn_w_up, ffn_w_down,
              ssm_w_in, ssm_conv_w, ssm_conv_b, ssm_dt_bias, ssm_a_log, ssm_d, ssm_norm_g, ssm_w_out,
              attn_w_qkv, attn_lambda, attn_subln_g, attn_w_out):
    b = x.shape[0]
    cond = jax.nn.silu(c)
    for i in range(DEPTH):
        mods = (cond @ ada_w[i] + ada_b[i]).reshape(b, N_SUB, 3, D_MODEL)
        gate = lambda s: (1 + mods[:, s, 2])[:, None, :]
        y = swiglu(modulate(x, mods[:, 0]), ffn_w_gate[i, 0], ffn_w_up[i, 0], ffn_w_down[i, 0])
        x = layer_norm(ALPHA * x + 0.5 * gate(0) * y, ln_g[i, 0], ln_b[i, 0])
        u = modulate(x, mods[:, 1])
        li = i // N_MIXERS
        if i % N_MIXERS == 0:
            y = mamba2_bidir(u, ssm_w_in[li], ssm_conv_w[li], ssm_conv_b[li], ssm_dt_bias[li],
                             ssm_a_log[li], ssm_d[li], ssm_norm_g[li], ssm_w_out[li])
        else:
            y = diff_attention(u, attn_w_qkv[li], attn_lambda[li], attn_subln_g[li], attn_w_out[li],
                               lambda_init_fn(i))
        x = layer_norm(ALPHA * x + gate(1) * y, ln_g[i, 1], ln_b[i, 1])
        y = swiglu(modulate(x, mods[:, 2]), ffn_w_gate[i, 1], ffn_w_up[i, 1], ffn_w_down[i, 1])
        x = layer_norm(ALPHA * x + 0.5 * gate(2) * y, ln_g[i, 2], ln_b[i, 2])
    return x
```

```python
import functools
import math

import jax
import jax.numpy as jnp
from jax import lax
from jax.experimental import pallas as pl
from jax.experimental.pallas import tpu as pltpu

F32 = jnp.float32
BF16 = jnp.bfloat16
LN_EPS = 1e-5
N_SUB = 3
N_MIXERS = 2
SSM_GROUPS = 4
SSM_STATE = 128
SSM_HEADDIM = 64
SSD_CHUNK = 128
DA_HEADS = 8
LANES = 128
HALO = 8
VMEM_LIMIT = 56 * 1024 * 1024
NEG_BIG = -1e30


def _cparams(*sem):
    return pltpu.CompilerParams(dimension_semantics=sem, vmem_limit_bytes=VMEM_LIMIT)


def _resident(shape, index_map):
    return pl.BlockSpec(shape, index_map, pipeline_mode=pl.Buffered(1))


def _dot(a, b):
    return jnp.dot(a, b, preferred_element_type=F32)


def _dot_nt(a, b):
    return lax.dot_general(a, b, (((1,), (1,)), ((), ())), preferred_element_type=F32)


def _split3(a):
    hi = a.astype(BF16)
    r = a - hi.astype(F32)
    mid = r.astype(BF16)
    lo = (r - mid.astype(F32)).astype(BF16)
    return hi, mid, lo


def _silu(v):
    return v * jax.nn.sigmoid(v)


def _layer_norm(z, g, b):
    mu = jnp.mean(z, axis=-1, keepdims=True)
    zc = z - mu
    var = jnp.mean(zc * zc, axis=-1, keepdims=True)
    return zc * lax.rsqrt(var + LN_EPS) * g + b


def _modulate(x, mod_ref):
    return x * (1.0 + mod_ref[0, 1:2, :]) + mod_ref[0, 0:1, :]


def _adaln_kernel(c_ref, w_ref, b_ref, o_ref):
    cond = _silu(c_ref[...])
    c_hi, c_mid, c_lo = _split3(cond)
    w_hi, w_mid, w_lo = _split3(w_ref[0])
    acc = _dot(c_hi, w_hi) + (_dot(c_hi, w_mid) + _dot(c_mid, w_hi))
    acc = acc + (_dot(c_hi, w_lo) + _dot(c_mid, w_mid) + _dot(c_lo, w_hi))
    o_ref[0] = acc + b_ref[0]


def _adaln(c, ada_w, ada_b):
    depth, d, n = ada_w.shape
    b = c.shape[0]
    rows = -(-b // HALO) * HALO
    c_pad = jnp.zeros((rows, d), F32).at[:b].set(c)
    tn = 1024 if n % 1024 == 0 else n
    out = pl.pallas_call(
        _adaln_kernel,
        grid=(depth, n // tn),
        in_specs=[
            pl.BlockSpec((rows, d), lambda i, j: (0, 0)),
            pl.BlockSpec((1, d, tn), lambda i, j: (i, 0, j)),
            pl.BlockSpec((1, 1, tn), lambda i, j: (i, 0, j)),
        ],
        out_specs=pl.BlockSpec((1, rows, tn), lambda i, j: (i, 0, j)),
        out_shape=jax.ShapeDtypeStruct((depth, rows, n), F32),
        compiler_params=_cparams("arbitrary", "arbitrary"),
        name="adaln",
    )(c_pad, ada_w, ada_b.reshape(depth, 1, n))
    return out[:, :b].reshape(depth, b, N_SUB, 3, d)


def _ffn_kernel(x_ref, mod_ref, wg_ref, wu_ref, wd_ref, lng_ref, lnb_ref, o_ref, *, alpha):
    x = x_ref[...]
    u = _modulate(x, mod_ref).astype(BF16)
    g = _dot(u, wg_ref[...])
    v = _dot(u, wu_ref[...])
    h = (_silu(g) * v).astype(BF16)
    y = _dot(h, wd_ref[...])
    z = alpha * x + (0.5 * (1.0 + mod_ref[0, 2:3, :])) * y
    o_ref[...] = _layer_norm(z, lng_ref[...], lnb_ref[...])


def _row_tile(t, want):
    tm = min(want, t)
    assert t % tm == 0
    return tm


def _ffn(x, mod, wg, wu, wd, lng, lnb, *, seq, alpha, tm=512):
    n, d = x.shape
    f = wg.shape[1]
    tm = _row_tile(seq, tm)
    per_b = seq // tm
    return pl.pallas_call(
        functools.partial(_ffn_kernel, alpha=alpha),
        grid=(n // tm,),
        in_specs=[
            pl.BlockSpec((tm, d), lambda i: (i, 0)),
            pl.BlockSpec((1, 3, d), lambda i: (i // per_b, 0, 0)),
            _resident((d, f), lambda i: (0, 0)),
            _resident((d, f), lambda i: (0, 0)),
            _resident((f, d), lambda i: (0, 0)),
            _resident((1, d), lambda i: (0, 0)),
            _resident((1, d), lambda i: (0, 0)),
        ],
        out_specs=pl.BlockSpec((tm, d), lambda i: (i, 0)),
        out_shape=jax.ShapeDtypeStruct((n, d), F32),
        compiler_params=_cparams("arbitrary"),
        name="ffn_ln",
    )(x, mod, wg, wu, wd, lng.reshape(1, d), lnb.reshape(1, d))


def _proj_ln_kernel(a_ref, x_ref, mod_ref, w_ref, lng_ref, lnb_ref, o_ref, *, alpha):
    y = _dot(a_ref[...], w_ref[...])
    z = alpha * x_ref[...] + (1.0 + mod_ref[0, 2:3, :]) * y
    o_ref[...] = _layer_norm(z, lng_ref[...], lnb_ref[...])


def _proj_ln(a, x, mod, w, lng, lnb, *, seq, alpha, tm=512):
    n, d = x.shape
    k = a.shape[1]
    tm = _row_tile(seq, tm)
    per_b = seq // tm
    return pl.pallas_call(
        functools.partial(_proj_ln_kernel, alpha=alpha),
        grid=(n // tm,),
        in_specs=[
            pl.BlockSpec((tm, k), lambda i: (i, 0)),
            pl.BlockSpec((tm, d), lambda i: (i, 0)),
            pl.BlockSpec((1, 3, d), lambda i: (i // per_b, 0, 0)),
            _resident((k, d), lambda i: (0, 0)),
            _resident((1, d), lambda i: (0, 0)),
            _resident((1, d), lambda i: (0, 0)),
        ],
        out_specs=pl.BlockSpec((tm, d), lambda i: (i, 0)),
        out_shape=jax.ShapeDtypeStruct((n, d), F32),
        compiler_params=_cparams("arbitrary"),
        name="proj_ln",
    )(a, x, mod, w, lng.reshape(1, d), lnb.reshape(1, d))


def _ssm_in_kernel(x_ref, mod_ref, w_ref, z_ref, xbc_ref, dt_ref, *, d_inner, conv_ch):
    u = _modulate(x_ref[...], mod_ref).astype(BF16)
    p = _dot(u, w_ref[...])
    z_ref[...] = p[:, :d_inner]
    xbc_ref[...] = p[:, d_inner:d_inner + conv_ch]
    dt_ref[...] = p[:, d_inner + conv_ch:]


def _ssm_in(x, mod, w_pad, *, seq, d_inner, conv_ch, tm=512):
    n, d = x.shape
    wn = w_pad.shape[1]
    tm = _row_tile(seq, tm)
    per_b = seq // tm
    return pl.pallas_call(
        functools.partial(_ssm_in_kernel, d_inner=d_inner, conv_ch=conv_ch),
        grid=(n // tm,),
        in_specs=[
            pl.BlockSpec((tm, d), lambda i: (i, 0)),
            pl.BlockSpec((1, 3, d), lambda i: (i // per_b, 0, 0)),
            _resident((d, wn), lambda i: (0, 0)),
        ],
        out_specs=[
            pl.BlockSpec((tm, d_inner), lambda i: (i, 0)),
            pl.BlockSpec((tm, conv_ch), lambda i: (i, 0)),
            pl.BlockSpec((tm, LANES), lambda i: (i, 0)),
        ],
        out_shape=[
            jax.ShapeDtypeStruct((n, d_inner), F32),
            jax.ShapeDtypeStruct((n, conv_ch), F32),
            jax.ShapeDtypeStruct((n, LANES), F32),
        ],
        compiler_params=_cparams("arbitrary"),
        name="ssm_in",
    )(x, mod, w_pad)


def _ssd_kernel(*refs, reverse, final, n_heads, n_chunks):
    if final:
        (xbc_ref, hp_ref, hn_ref, dt_ref, cw_ref, cb_ref, dtb_ref, alog_ref,
         z_ref, yb_ref, dsk_ref, ng_ref, out_ref, state_ref, buf_ref, act_ref, y_ref) = refs
    else:
        (xbc_ref, hp_ref, hn_ref, dt_ref, cw_ref, cb_ref, dtb_ref, alog_ref,
         out_ref, state_ref, buf_ref, act_ref) = refs
    L = SSD_CHUNK
    P = SSM_HEADDIM
    G = SSM_GROUPS
    NS = SSM_STATE
    H = n_heads
    HPG = H // G
    d_inner = H * P
    conv_ch = d_inner + 2 * G * NS
    assert L == 2 * P == LANES and HPG % 2 == 0

    step = pl.program_id(1)
    chunk = (n_chunks - 1 - step) if reverse else step

    @pl.when(step == 0)
    def _():
        state_ref[...] = jnp.zeros_like(state_ref)

    buf_ref[0:HALO, :] = jnp.where(chunk == 0, 0.0, hp_ref[0])
    buf_ref[HALO:HALO + L, :] = xbc_ref[0]
    buf_ref[HALO + L:2 * HALO + L, :] = jnp.where(chunk == n_chunks - 1, 0.0, hn_ref[0])
    cblk = 512
    for c0 in range(0, conv_ch, cblk):
        cs_ = slice(c0, c0 + cblk)
        conv = (buf_ref[HALO - 1:HALO - 1 + L, cs_] * cw_ref[0:1, cs_]
                + buf_ref[HALO:HALO + L, cs_] * cw_ref[1:2, cs_]
                + buf_ref[HALO + 1:HALO + 1 + L, cs_] * cw_ref[2:3, cs_]
                + cb_ref[0:1, cs_])
        act_ref[:, cs_] = _silu(conv)

    lane0 = H if reverse else 0
    x_dt = dt_ref[0] + dtb_ref[...]
    dt = jnp.maximum(x_dt, 0.0) + jnp.log1p(jnp.exp(-jnp.abs(x_dt)))
    a = dt * (-jnp.exp(alog_ref[...]))
    ri = lax.broadcasted_iota(jnp.int32, (L, L), 0)
    ci = lax.broadcasted_iota(jnp.int32, (L, L), 1)
    mask = (ci >= ri) if reverse else (ci <= ri)
    tri = jnp.where(mask, 1.0, 0.0).astype(BF16)
    a_hi, a_mid, a_lo = _split3(a)
    cs = _dot(tri, a_hi) + (_dot(tri, a_mid) + _dot(tri, a_lo))
    csT = cs.T
    dtT = dt.T
    r_tot = 0 if reverse else L - 1
    wT = jnp.exp(csT[:, r_tot:r_tot + 1] - csT) * dtT
    lane = lax.broadcasted_iota(jnp.int32, (L, LANES), 1)
    lo_half = lane < P

    for g in range(G):
        b0 = d_inner + g * NS
        c0 = d_inner + G * NS + g * NS
        Bg = act_ref[:, b0:b0 + NS]
        Cg = act_ref[:, c0:c0 + NS].astype(BF16)
        CB = _dot_nt(Cg, Bg.astype(BF16))
        BgT = Bg.T
        yoff = _dot(Cg, state_ref[g].astype(BF16))
        for k in range(HPG // 2):
            h0 = g * HPG + 2 * k
            ms, bws, es = [], [], []
            for h in (h0, h0 + 1):
                hl = lane0 + h
                colb = jnp.broadcast_to(cs[:, hl:hl + 1], (L, L))
                seg = colb - csT[hl:hl + 1, :]
                lm = jnp.exp(jnp.where(mask, seg, NEG_BIG))
                ms.append((CB * lm * dtT[hl:hl + 1, :]).astype(BF16))
                bws.append((BgT * wT[hl:hl + 1, :]).astype(BF16))
                es.append(jnp.exp(colb))
            lhs = jnp.concatenate(
                [jnp.concatenate(ms, axis=1), jnp.concatenate(bws, axis=1)], axis=0)
            xp = act_ref[:, h0 * P:(h0 + 2) * P]
            rhs = jnp.concatenate(
                [jnp.where(lo_half, xp, 0.0), jnp.where(lo_half, 0.0, xp)], axis=0).astype(BF16)
            res = _dot(lhs, rhs)
            e_pair = jnp.where(lo_half, es[0], es[1])
            lsl = slice(2 * k * P, (2 * k + 2) * P)
            y_pair = res[:L] + e_pair * yoff[:, lsl]
            state_ref[g, :, lsl] = state_ref[g, :, lsl] * e_pair[r_tot:r_tot + 1, :] + res[L:]
            osl = slice(h0 * P, (h0 + 2) * P)
            if final:
                y_ref[:, osl] = y_pair + yb_ref[0, :, osl] + xp * dsk_ref[0:1, osl]
            else:
                out_ref[0, :, osl] = y_pair

    if final:
        gw = d_inner // G
        for g in range(G):
            gs = slice(g * gw, (g + 1) * gw)
            y = y_ref[:, gs] * _silu(z_ref[0, :, gs])
            ms_ = jnp.mean(y * y, axis=-1, keepdims=True)
            out_ref[0, :, gs] = (y * lax.rsqrt(ms_ + LN_EPS) * ng_ref[0:1, gs]).astype(out_ref.dtype)


def _ssd(xbc, dt, z, yb, conv_w, conv_b, dt_bias_l, a_log_l, d_skip_l, norm_g, *, reverse, final, n_heads):
    b, t, conv_ch = xbc.shape
    L = SSD_CHUNK
    nc = t // L
    d_inner = n_heads * SSM_HEADDIM
    hb = L // HALO
    n_hb = t // HALO

    def cidx(c):
        return (nc - 1 - c) if reverse else c

    in_specs = [
        pl.BlockSpec((1, L, conv_ch), lambda i, c: (i, cidx(c), 0)),
        pl.BlockSpec((1, HALO, conv_ch), lambda i, c: (i, jnp.maximum(cidx(c) * hb - 1, 0), 0)),
        pl.BlockSpec((1, HALO, conv_ch), lambda i, c: (i, jnp.minimum((cidx(c) + 1) * hb, n_hb - 1), 0)),
        pl.BlockSpec((1, L, LANES), lambda i, c: (i, cidx(c), 0)),
        _resident((3, conv_ch), lambda i, c: (0, 0)),
        _resident((1, conv_ch), lambda i, c: (0, 0)),
        _resident((1, LANES), lambda i, c: (0, 0)),
        _resident((1, LANES), lambda i, c: (0, 0)),
    ]
    args = [xbc, xbc, xbc, dt, conv_w, conv_b.reshape(1, conv_ch), dt_bias_l, a_log_l]
    scratch = [
        pltpu.VMEM((SSM_GROUPS, SSM_STATE, d_inner // SSM_GROUPS), F32),
        pltpu.VMEM((L + 2 * HALO, conv_ch), F32),
        pltpu.VMEM((L, conv_ch), F32),
    ]
    if final:
        in_specs += [
            pl.BlockSpec((1, L, d_inner), lambda i, c: (i, cidx(c), 0)),
            pl.BlockSpec((1, L, d_inner), lambda i, c: (i, cidx(c), 0)),
            _resident((1, d_inner), lambda i, c: (0, 0)),
            _resident((1, d_inner), lambda i, c: (0, 0)),
        ]
        args += [z, yb, d_skip_l, norm_g.reshape(1, d_inner)]
        scratch.append(pltpu.VMEM((L, d_inner), F32))
        out_dtype = BF16
    else:
        out_dtype = F32
    return pl.pallas_call(
        functools.partial(_ssd_kernel, reverse=reverse, final=final, n_heads=n_heads, n_chunks=nc),
        grid=(b, nc),
        in_specs=in_specs,
        out_specs=pl.BlockSpec((1, L, d_inner), lambda i, c: (i, cidx(c), 0)),
        out_shape=jax.ShapeDtypeStruct((b, t, d_inner), out_dtype),
        scratch_shapes=scratch,
        compiler_params=_cparams("arbitrary", "arbitrary"),
        name="ssd_fwd_final" if final else "ssd_bwd",
    )(*args)


def _lane_pack(v2, n_heads):
    flat = v2.reshape(1, 2 * n_heads).astype(F32)
    return jnp.zeros((1, LANES), F32).at[:, :2 * n_heads].set(flat)


def _mamba2_bidir(x, mod, w_in, conv_w, conv_b, dt_bias, a_log, d_skip, norm_g, *, batch, seq):
    d = x.shape[1]
    n_heads = d_skip.shape[0]
    d_inner = n_heads * SSM_HEADDIM
    conv_ch = conv_w.shape[1]
    assert 2 * n_heads <= LANES
    n_in = w_in.shape[1]
    w_pad = jnp.zeros((d, d_inner + conv_ch + LANES), BF16).at[:, :n_in].set(w_in.astype(BF16))
    z, xbc, dt = _ssm_in(x, mod, w_pad, seq=seq, d_inner=d_inner, conv_ch=conv_ch)
    z = z.reshape(batch, seq, d_inner)
    xbc = xbc.reshape(batch, seq, conv_ch)
    dt = dt.reshape(batch, seq, LANES)
    dtb = _lane_pack(dt_bias, n_heads)
    alog = _lane_pack(a_log, n_heads)
    dsk = jnp.repeat(d_skip.astype(F32), SSM_HEADDIM).reshape(1, d_inner)
    common = dict(n_heads=n_heads)
    yb = _ssd(xbc, dt, None, None, conv_w, conv_b, dtb, alog, None, None, reverse=True, final=False, **common)
    y = _ssd(xbc, dt, z, yb, conv_w, conv_b, dtb, alog, dsk, norm_g, reverse=False, final=True, **common)
    return y.reshape(batch * seq, d_inner)


def _qkv_kernel(x_ref, mod_ref, w_ref, qT_ref, k_ref, vT_ref, *, n_heads, scale):
    u = _modulate(x_ref[...], mod_ref).astype(BF16)
    p = _dot(u, w_ref[...])
    hq = n_heads * LANES
    for h in range(n_heads):
        hs = slice(h * LANES, (h + 1) * LANES)
        qT_ref[0, h] = (p[:, hs] * scale).T.astype(BF16)
        k_ref[0, h] = p[:, hq + h * LANES:hq + (h + 1) * LANES].astype(BF16)
        vT_ref[0, h] = p[:, 2 * hq + h * LANES:2 * hq + (h + 1) * LANES].T.astype(BF16)


def _qkv(x, mod, w, *, batch, seq, n_heads, scale, tm=512):
    n, d = x.shape
    wn = w.shape[1]
    tm = _row_tile(seq, tm)
    per_b = seq // tm
    t_spec = pl.BlockSpec((1, n_heads, LANES, tm), lambda i: (i // per_b, 0, 0, i % per_b))
    return pl.pallas_call(
        functools.partial(_qkv_kernel, n_heads=n_heads, scale=scale),
        grid=(n // tm,),
        in_specs=[
            pl.BlockSpec((tm, d), lambda i: (i, 0)),
            pl.BlockSpec((1, 3, d), lambda i: (i // per_b, 0, 0)),
            _resident((d, wn), lambda i: (0, 0)),
        ],
        out_specs=[
            t_spec,
            pl.BlockSpec((1, n_heads, tm, LANES), lambda i: (i // per_b, 0, i % per_b, 0)),
            t_spec,
        ],
        out_shape=[
            jax.ShapeDtypeStruct((batch, n_heads, LANES, seq), BF16),
            jax.ShapeDtypeStruct((batch, n_heads, seq, LANES), BF16),
            jax.ShapeDtypeStruct((batch, n_heads, LANES, seq), BF16),
        ],
        compiler_params=_cparams("arbitrary"),
        name="qkv",
    )(x, mod, w)


def _attn_kernel(slope_ref, lam_ref, g_ref, qT_ref, k_ref, vT_ref, o_ref, s_ref, *, seq, tq, tk, dh, lambda_init):
    h = pl.program_id(1)
    i = pl.program_id(2)
    slope = slope_ref[h]
    lam = lam_ref[...]
    lam_full = (jnp.exp(jnp.sum(lam[0:1] * lam[1:2], axis=-1, keepdims=True))
                - jnp.exp(jnp.sum(lam[2:3] * lam[3:4], axis=-1, keepdims=True)) + lambda_init)
    qT = qT_ref[0, 0]
    row = lax.broadcasted_iota(jnp.int32, qT.shape, 0)
    zero = jnp.zeros_like(qT)
    qTs = (jnp.where(row < dh, qT, zero), jnp.where(row >= dh, qT, zero))
    kpos = lax.broadcasted_iota(jnp.int32, (tk, tq), 0)
    qpos = lax.broadcasted_iota(jnp.int32, (tk, tq), 1) + i * tq
    nk = seq // tk
    outs = []
    for m in range(2):
        mx = jnp.full((1, tq), NEG_BIG, F32)
        for j in range(nk):
            sT = _dot(k_ref[0, 0, j * tk:(j + 1) * tk, :], qTs[m])
            dist = jnp.abs(kpos + (j * tk) - qpos).astype(F32)
            sT = sT - slope * dist
            s_ref[j * tk:(j + 1) * tk, :] = sT
            mx = jnp.maximum(mx, jnp.max(sT, axis=0, keepdims=True))
        acc = jnp.zeros((LANES, tq), F32)
        den = jnp.zeros((1, tq), F32)
        for j in range(nk):
            p = jnp.exp(s_ref[j * tk:(j + 1) * tk, :] - mx)
            den = den + jnp.sum(p, axis=0, keepdims=True)
            acc = acc + _dot(vT_ref[0, 0, :, j * tk:(j + 1) * tk], p.astype(BF16))
        outs.append(acc / den)
    oT = outs[0] - lam_full * outs[1]
    ms = jnp.mean(oT * oT, axis=0, keepdims=True)
    on = oT * lax.rsqrt(ms + LN_EPS) * g_ref[...]
    o_ref[...] = on.T.astype(o_ref.dtype)


def _diff_attention(x, mod, w_qkv, lam, subln_g, *, batch, seq, lambda_init, tq=256, tk=512):
    n_heads = DA_HEADS
    dh = lam.shape[1]
    assert 2 * dh == LANES and w_qkv.shape[1] == 3 * n_heads * LANES
    qT, k, vT = _qkv(x, mod, w_qkv.astype(BF16), batch=batch, seq=seq, n_heads=n_heads, scale=dh ** -0.5)
    tq = _row_tile(seq, tq)
    tk = _row_tile(seq, tk)
    nq = seq // tq
    slopes = jnp.asarray([2.0 ** (-8.0 * (h + 1) / n_heads) for h in range(n_heads)], F32)
    g_col = (subln_g.astype(F32) * (1.0 - lambda_init)).reshape(LANES, 1)
    return pl.pallas_call(
        functools.partial(_attn_kernel, seq=seq, tq=tq, tk=tk, dh=dh, lambda_init=lambda_init),
        grid=(batch, n_heads, nq),
        in_specs=[
            pl.BlockSpec(memory_space=pltpu.SMEM),
            _resident((4, dh), lambda b, h, i: (0, 0)),
            _resident((LANES, 1), lambda b, h, i: (0, 0)),
            pl.BlockSpec((1, 1, LANES, tq), lambda b, h, i: (b, h, 0, i)),
            pl.BlockSpec((1, 1, seq, LANES), lambda b, h, i: (b, h, 0, 0)),
            pl.BlockSpec((1, 1, LANES, seq), lambda b, h, i: (b, h, 0, 0)),
        ],
        out_specs=pl.BlockSpec((tq, LANES), lambda b, h, i: (b * nq + i, h)),
        out_shape=jax.ShapeDtypeStruct((batch * seq, n_heads * LANES), BF16),
        scratch_shapes=[pltpu.VMEM((seq, tq), F32)],
        compiler_params=_cparams("arbitrary", "arbitrary", "arbitrary"),
        name="diff_attn",
    )(slopes, lam.astype(F32), g_col, qT, k, vT)


def kernel(x, c, ada_w, ada_b, ln_g, ln_b, ffn_w_gate, ffn_w_up, ffn_w_down, ssm_w_in, ssm_conv_w, ssm_conv_b, ssm_dt_bias, ssm_a_log, ssm_d, ssm_norm_g, ssm_w_out, attn_w_qkv, attn_lambda, attn_subln_g, attn_w_out):
    batch, seq, d = x.shape
    depth = ada_w.shape[0]
    alpha = (2 * depth) ** 0.25
    mods = _adaln(c, ada_w, ada_b)
    xf = x.reshape(batch * seq, d)
    for i in range(depth):
        ffn = functools.partial(_ffn, seq=seq, alpha=alpha)
        xf = ffn(xf, mods[i, :, 0], ffn_w_gate[i, 0].astype(BF16), ffn_w_up[i, 0].astype(BF16),
                 ffn_w_down[i, 0].astype(BF16), ln_g[i, 0], ln_b[i, 0])
        li = i // N_MIXERS
        if i % N_MIXERS == 0:
            y = _mamba2_bidir(xf, mods[i, :, 1], ssm_w_in[li], ssm_conv_w[li], ssm_conv_b[li], ssm_dt_bias[li],
                              ssm_a_log[li], ssm_d[li], ssm_norm_g[li], batch=batch, seq=seq)
            w_out = ssm_w_out[li]
        else:
            lambda_init = 0.8 - 0.6 * math.exp(-0.3 * i)
            y = _diff_attention(xf, mods[i, :, 1], attn_w_qkv[li], attn_lambda[li], attn_subln_g[li],
                                batch=batch, seq=seq, lambda_init=lambda_init)
            w_out = attn_w_out[li]
        xf = _proj_ln(y, xf, mods[i, :, 1], w_out.astype(BF16), ln_g[i, 1], ln_b[i, 1], seq=seq, alpha=alpha)
        xf = ffn(xf, mods[i, :, 2], ffn_w_gate[i, 1].astype(BF16), ffn_w_up[i, 1].astype(BF16),
                 ffn_w_down[i, 1].astype(BF16), ln_g[i, 2], ln_b[i, 2])
    return xf.reshape(batch, seq, d)
```

```python
import functools
import math

import jax
import jax.numpy as jnp
import numpy as np
from jax import lax
from jax.experimental import pallas as pl
from jax.experimental.pallas import tpu as pltpu

F32 = jnp.float32
BF16 = jnp.bfloat16
LN_EPS = 1e-5
N_SUB = 3
N_MIXERS = 2
SSM_GROUPS = 4
SSM_STATE = 128
SSM_HEADDIM = 64
SSD_CHUNK = 128
DA_HEADS = 8
LANES = 128
HALO = 8
VMEM_LIMIT = 56 * 1024 * 1024
NEG_BIG = -1e30


def _cparams(*sem):
    return pltpu.CompilerParams(dimension_semantics=sem, vmem_limit_bytes=VMEM_LIMIT)


def _resident(shape, index_map):
    return pl.BlockSpec(shape, index_map, pipeline_mode=pl.Buffered(1))


def _dot(a, b):
    return jnp.dot(a, b, preferred_element_type=F32)


def _dot_nt(a, b):
    return lax.dot_general(a, b, (((1,), (1,)), ((), ())), preferred_element_type=F32)


def _split3(a):
    hi = a.astype(BF16)
    r = a - hi.astype(F32)
    mid = r.astype(BF16)
    lo = (r - mid.astype(F32)).astype(BF16)
    return hi, mid, lo


def _silu(v):
    return v * jax.nn.sigmoid(v)


def _layer_norm(z, g, b):
    mu = jnp.mean(z, axis=-1, keepdims=True)
    zc = z - mu
    var = jnp.mean(zc * zc, axis=-1, keepdims=True)
    return zc * lax.rsqrt(var + LN_EPS) * g + b


def _modulate(x, mod_ref):
    return x * (1.0 + mod_ref[0, 1:2, :]) + mod_ref[0, 0:1, :]


def _adaln_kernel(c_ref, w_ref, b_ref, o_ref):
    cond = _silu(c_ref[...])
    c_hi, c_mid, c_lo = _split3(cond)
    w_hi, w_mid, w_lo = _split3(w_ref[0])
    acc = _dot(c_hi, w_hi) + (_dot(c_hi, w_mid) + _dot(c_mid, w_hi))
    acc = acc + (_dot(c_hi, w_lo) + _dot(c_mid, w_mid) + _dot(c_lo, w_hi))
    o_ref[0] = acc + b_ref[0]


def _adaln(c, ada_w, ada_b):
    depth, d, n = ada_w.shape
    b = c.shape[0]
    rows = -(-b // HALO) * HALO
    c_pad = jnp.zeros((rows, d), F32).at[:b].set(c)
    tn = 1024 if n % 1024 == 0 else n
    out = pl.pallas_call(
        _adaln_kernel,
        grid=(depth, n // tn),
        in_specs=[
            pl.BlockSpec((rows, d), lambda i, j: (0, 0)),
            pl.BlockSpec((1, d, tn), lambda i, j: (i, 0, j)),
            pl.BlockSpec((1, 1, tn), lambda i, j: (i, 0, j)),
        ],
        out_specs=pl.BlockSpec((1, rows, tn), lambda i, j: (i, 0, j)),
        out_shape=jax.ShapeDtypeStruct((depth, rows, n), F32),
        compiler_params=_cparams("arbitrary", "arbitrary"),
        name="adaln",
    )(c_pad, ada_w, ada_b.reshape(depth, 1, n))
    return out[:, :b].reshape(depth, b, N_SUB, 3, d)


def _ffn_kernel(x_ref, mod_ref, wg_ref, wu_ref, wd_ref, lng_ref, lnb_ref, o_ref, *, alpha):
    x = x_ref[...]
    u = _modulate(x, mod_ref).astype(BF16)
    g = _dot(u, wg_ref[...])
    v = _dot(u, wu_ref[...])
    h = (_silu(g) * v).astype(BF16)
    y = _dot(h, wd_ref[...])
    z = alpha * x + (0.5 * (1.0 + mod_ref[0, 2:3, :])) * y
    o_ref[...] = _layer_norm(z, lng_ref[...], lnb_ref[...])


def _row_tile(t, want):
    tm = min(want, t)
    assert t % tm == 0
    return tm


def _ffn(x, mod, wg, wu, wd, lng, lnb, *, seq, alpha, tm=512):
    n, d = x.shape
    f = wg.shape[1]
    tm = _row_tile(seq, tm)
    per_b = seq // tm
    return pl.pallas_call(
        functools.partial(_ffn_kernel, alpha=alpha),
        grid=(n // tm,),
        in_specs=[
            pl.BlockSpec((tm, d), lambda i: (i, 0)),
            pl.BlockSpec((1, 3, d), lambda i: (i // per_b, 0, 0)),
            _resident((d, f), lambda i: (0, 0)),
            _resident((d, f), lambda i: (0, 0)),
            _resident((f, d), lambda i: (0, 0)),
            _resident((1, d), lambda i: (0, 0)),
            _resident((1, d), lambda i: (0, 0)),
        ],
        out_specs=pl.BlockSpec((tm, d), lambda i: (i, 0)),
        out_shape=jax.ShapeDtypeStruct((n, d), F32),
        compiler_params=_cparams("arbitrary"),
        name="ffn_ln",
    )(x, mod, wg, wu, wd, lng.reshape(1, d), lnb.reshape(1, d))


def _proj_ln_kernel(a_ref, x_ref, mod_ref, w_ref, lng_ref, lnb_ref, o_ref, *, alpha):
    y = _dot(a_ref[...], w_ref[...])
    z = alpha * x_ref[...] + (1.0 + mod_ref[0, 2:3, :]) * y
    o_ref[...] = _layer_norm(z, lng_ref[...], lnb_ref[...])


def _proj_ln(a, x, mod, w, lng, lnb, *, seq, alpha, tm=512):
    n, d = x.shape
    k = a.shape[1]
    tm = _row_tile(seq, tm)
    per_b = seq // tm
    return pl.pallas_call(
        functools.partial(_proj_ln_kernel, alpha=alpha),
        grid=(n // tm,),
        in_specs=[
            pl.BlockSpec((tm, k), lambda i: (i, 0)),
            pl.BlockSpec((tm, d), lambda i: (i, 0)),
            pl.BlockSpec((1, 3, d), lambda i: (i // per_b, 0, 0)),
            _resident((k, d), lambda i: (0, 0)),
            _resident((1, d), lambda i: (0, 0)),
            _resident((1, d), lambda i: (0, 0)),
        ],
        out_specs=pl.BlockSpec((tm, d), lambda i: (i, 0)),
        out_shape=jax.ShapeDtypeStruct((n, d), F32),
        compiler_params=_cparams("arbitrary"),
        name="proj_ln",
    )(a, x, mod, w, lng.reshape(1, d), lnb.reshape(1, d))


def _ssm_in_kernel(x_ref, mod_ref, w_ref, z_ref, xbc_ref, dt_ref, *, d_inner, conv_ch):
    u = _modulate(x_ref[...], mod_ref).astype(BF16)
    p = _dot(u, w_ref[...])
    z_ref[...] = p[:, :d_inner]
    xbc_ref[...] = p[:, d_inner:d_inner + conv_ch]
    dt_ref[...] = p[:, d_inner + conv_ch:]


def _ssm_in(x, mod, w_pad, *, seq, d_inner, conv_ch, tm=512):
    n, d = x.shape
    wn = w_pad.shape[1]
    tm = _row_tile(seq, tm)
    per_b = seq // tm
    return pl.pallas_call(
        functools.partial(_ssm_in_kernel, d_inner=d_inner, conv_ch=conv_ch),
        grid=(n // tm,),
        in_specs=[
            pl.BlockSpec((tm, d), lambda i: (i, 0)),
            pl.BlockSpec((1, 3, d), lambda i: (i // per_b, 0, 0)),
            _resident((d, wn), lambda i: (0, 0)),
        ],
        out_specs=[
            pl.BlockSpec((tm, d_inner), lambda i: (i, 0)),
            pl.BlockSpec((tm, conv_ch), lambda i: (i, 0)),
            pl.BlockSpec((tm, LANES), lambda i: (i, 0)),
        ],
        out_shape=[
            jax.ShapeDtypeStruct((n, d_inner), F32),
            jax.ShapeDtypeStruct((n, conv_ch), F32),
            jax.ShapeDtypeStruct((n, LANES), F32),
        ],
        compiler_params=_cparams("arbitrary"),
        name="ssm_in",
    )(x, mod, w_pad)


def _ssd_kernel(*refs, reverse, final, n_heads, n_chunks):
    if final:
        (act_ref, dt_ref, dtb_ref, alog_ref, z_ref, yb_ref, dsk_ref, ng_ref,
         out_ref, state_ref, y_ref) = refs
    else:
        (xbc_ref, hp_ref, hn_ref, dt_ref, cw_ref, cb_ref, dtb_ref, alog_ref,
         out_ref, act_ref, state_ref, buf_ref) = refs
    L = SSD_CHUNK
    P = SSM_HEADDIM
    G = SSM_GROUPS
    NS = SSM_STATE
    H = n_heads
    HPG = H // G
    d_inner = H * P
    conv_ch = d_inner + 2 * G * NS
    assert L == 2 * P == LANES and HPG % 2 == 0

    step = pl.program_id(1)
    chunk = (n_chunks - 1 - step) if reverse else step

    @pl.when(step == 0)
    def _():
        state_ref[...] = jnp.zeros_like(state_ref)

    if not final:
        buf_ref[0:HALO, :] = jnp.where(chunk == 0, 0.0, hp_ref[0])
        buf_ref[HALO:HALO + L, :] = xbc_ref[0]
        buf_ref[HALO + L:2 * HALO + L, :] = jnp.where(chunk == n_chunks - 1, 0.0, hn_ref[0])
        cblk = 512
        for c0 in range(0, conv_ch, cblk):
            cs_ = slice(c0, c0 + cblk)
            conv = (buf_ref[HALO - 1:HALO - 1 + L, cs_] * cw_ref[0:1, cs_]
                    + buf_ref[HALO:HALO + L, cs_] * cw_ref[1:2, cs_]
                    + buf_ref[HALO + 1:HALO + 1 + L, cs_] * cw_ref[2:3, cs_]
                    + cb_ref[0:1, cs_])
            act_ref[0, :, cs_] = _silu(conv)

    lane0 = H if reverse else 0
    x_dt = dt_ref[0] + dtb_ref[...]
    dt = jnp.maximum(x_dt, 0.0) + jnp.log1p(jnp.exp(-jnp.abs(x_dt)))
    a = dt * (-jnp.exp(alog_ref[...]))
    ri = lax.broadcasted_iota(jnp.int32, (L, L), 0)
    ci = lax.broadcasted_iota(jnp.int32, (L, L), 1)
    mask = (ci >= ri) if reverse else (ci <= ri)
    tri = jnp.where(mask, 1.0, 0.0).astype(BF16)
    a_hi, a_mid, a_lo = _split3(a)
    cs = _dot(tri, a_hi) + (_dot(tri, a_mid) + _dot(tri, a_lo))
    csT = cs.T
    dtT = dt.T
    r_tot = 0 if reverse else L - 1
    wT = jnp.exp(csT[:, r_tot:r_tot + 1] - csT) * dtT
    lane = lax.broadcasted_iota(jnp.int32, (L, LANES), 1)
    lo_half = lane < P

    for g in range(G):
        b0 = d_inner + g * NS
        c0 = d_inner + G * NS + g * NS
        Bg = act_ref[0, :, b0:b0 + NS]
        Cg = act_ref[0, :, c0:c0 + NS].astype(BF16)
        CB = _dot_nt(Cg, Bg.astype(BF16))
        BgT = Bg.T
        yoff = _dot(Cg, state_ref[g].astype(BF16))
        for k in range(HPG // 2):
            h0 = g * HPG + 2 * k
            ms, bws, es = [], [], []
            for h in (h0, h0 + 1):
                hl = lane0 + h
                colb = jnp.broadcast_to(cs[:, hl:hl + 1], (L, L))
                seg = colb - csT[hl:hl + 1, :]
                lm = jnp.exp(jnp.where(mask, seg, NEG_BIG))
                ms.append((CB * lm * dtT[hl:hl + 1, :]).astype(BF16))
                bws.append((BgT * wT[hl:hl + 1, :]).astype(BF16))
                es.append(jnp.exp(colb))
            lhs = jnp.concatenate(
                [jnp.concatenate(ms, axis=1), jnp.concatenate(bws, axis=1)], axis=0)
            xp = act_ref[0, :, h0 * P:(h0 + 2) * P]
            rhs = jnp.concatenate(
                [jnp.where(lo_half, xp, 0.0), jnp.where(lo_half, 0.0, xp)], axis=0).astype(BF16)
            res = _dot(lhs, rhs)
            e_pair = jnp.where(lo_half, es[0], es[1])
            lsl = slice(2 * k * P, (2 * k + 2) * P)
            y_pair = res[:L] + e_pair * yoff[:, lsl]
            state_ref[g, :, lsl] = state_ref[g, :, lsl] * e_pair[r_tot:r_tot + 1, :] + res[L:]
            osl = slice(h0 * P, (h0 + 2) * P)
            if final:
                y_ref[:, osl] = y_pair + yb_ref[0, :, osl] + xp * dsk_ref[0:1, osl]
            else:
                out_ref[0, :, osl] = y_pair

    if final:
        gw = d_inner // G
        for g in range(G):
            gs = slice(g * gw, (g + 1) * gw)
            y = y_ref[:, gs] * _silu(z_ref[0, :, gs])
            ms_ = jnp.mean(y * y, axis=-1, keepdims=True)
            out_ref[0, :, gs] = (y * lax.rsqrt(ms_ + LN_EPS) * ng_ref[0:1, gs]).astype(out_ref.dtype)


def _ssd_first(xbc, dt, conv_w, conv_b, dt_bias_l, a_log_l, *, n_heads):
    b, t, conv_ch = xbc.shape
    L = SSD_CHUNK
    nc = t // L
    d_inner = n_heads * SSM_HEADDIM
    hb = L // HALO
    n_hb = t // HALO

    def cidx(c):
        return nc - 1 - c

    return pl.pallas_call(
        functools.partial(_ssd_kernel, reverse=True, final=False, n_heads=n_heads, n_chunks=nc),
        grid=(b, nc),
        in_specs=[
            pl.BlockSpec((1, L, conv_ch), lambda i, c: (i, cidx(c), 0)),
            pl.BlockSpec((1, HALO, conv_ch), lambda i, c: (i, jnp.maximum(cidx(c) * hb - 1, 0), 0)),
            pl.BlockSpec((1, HALO, conv_ch), lambda i, c: (i, jnp.minimum((cidx(c) + 1) * hb, n_hb - 1), 0)),
            pl.BlockSpec((1, L, LANES), lambda i, c: (i, cidx(c), 0)),
            _resident((3, conv_ch), lambda i, c: (0, 0)),
            _resident((1, conv_ch), lambda i, c: (0, 0)),
            _resident((1, LANES), lambda i, c: (0, 0)),
            _resident((1, LANES), lambda i, c: (0, 0)),
        ],
        out_specs=[
            pl.BlockSpec((1, L, d_inner), lambda i, c: (i, cidx(c), 0)),
            pl.BlockSpec((1, L, conv_ch), lambda i, c: (i, cidx(c), 0)),
        ],
        out_shape=[
            jax.ShapeDtypeStruct((b, t, d_inner), F32),
            jax.ShapeDtypeStruct((b, t, conv_ch), F32),
        ],
        scratch_shapes=[
            pltpu.VMEM((SSM_GROUPS, SSM_STATE, d_inner // SSM_GROUPS), F32),
            pltpu.VMEM((L + 2 * HALO, conv_ch), F32),
        ],
        compiler_params=_cparams("arbitrary", "arbitrary"),
        name="ssd_bwd",
    )(xbc, xbc, xbc, dt, conv_w, conv_b.reshape(1, conv_ch), dt_bias_l, a_log_l)


def _ssd_final(act, dt, z, yb, dt_bias_l, a_log_l, d_skip_l, norm_g, *, n_heads):
    b, t, conv_ch = act.shape
    L = SSD_CHUNK
    nc = t // L
    d_inner = n_heads * SSM_HEADDIM

    def blk(w):
        return pl.BlockSpec((1, L, w), lambda i, c: (i, c, 0))

    return pl.pallas_call(
        functools.partial(_ssd_kernel, reverse=False, final=True, n_heads=n_heads, n_chunks=nc),
        grid=(b, nc),
        in_specs=[
            blk(conv_ch),
            blk(LANES),
            _resident((1, LANES), lambda i, c: (0, 0)),
            _resident((1, LANES), lambda i, c: (0, 0)),
            blk(d_inner),
            blk(d_inner),
            _resident((1, d_inner), lambda i, c: (0, 0)),
            _resident((1, d_inner), lambda i, c: (0, 0)),
        ],
        out_specs=blk(d_inner),
        out_shape=jax.ShapeDtypeStruct((b, t, d_inner), BF16),
        scratch_shapes=[
            pltpu.VMEM((SSM_GROUPS, SSM_STATE, d_inner // SSM_GROUPS), F32),
            pltpu.VMEM((L, d_inner), F32),
        ],
        compiler_params=_cparams("arbitrary", "arbitrary"),
        name="ssd_fwd_final",
    )(act, dt, dt_bias_l, a_log_l, z, yb, d_skip_l, norm_g.reshape(1, d_inner))


def _lane_pack(v2, n_heads):
    flat = v2.reshape(1, 2 * n_heads).astype(F32)
    return jnp.zeros((1, LANES), F32).at[:, :2 * n_heads].set(flat)


def _mamba2_bidir(x, mod, w_in, conv_w, conv_b, dt_bias, a_log, d_skip, norm_g, *, batch, seq):
    d = x.shape[1]
    n_heads = d_skip.shape[0]
    d_inner = n_heads * SSM_HEADDIM
    conv_ch = conv_w.shape[1]
    assert 2 * n_heads <= LANES
    n_in = w_in.shape[1]
    w_pad = jnp.zeros((d, d_inner + conv_ch + LANES), BF16).at[:, :n_in].set(w_in.astype(BF16))
    z, xbc, dt = _ssm_in(x, mod, w_pad, seq=seq, d_inner=d_inner, conv_ch=conv_ch)
    z = z.reshape(batch, seq, d_inner)
    xbc = xbc.reshape(batch, seq, conv_ch)
    dt = dt.reshape(batch, seq, LANES)
    dtb = _lane_pack(dt_bias, n_heads)
    alog = _lane_pack(a_log, n_heads)
    dsk = jnp.repeat(d_skip.astype(F32), SSM_HEADDIM).reshape(1, d_inner)
    yb, act = _ssd_first(xbc, dt, conv_w, conv_b, dtb, alog, n_heads=n_heads)
    y = _ssd_final(act, dt, z, yb, dtb, alog, dsk, norm_g, n_heads=n_heads)
    return y.reshape(batch * seq, d_inner)


ALIBI_BLOCK = 256
N_EXT = 8
LOG2E = math.log2(math.e)


def _alibi_consts(n_heads):
    out = []
    for h in range(n_heads):
        c = 2.0 ** (-8.0 * (h + 1) / n_heads) * LOG2E
        hi = float(np.float32(c).astype(jnp.bfloat16))
        lo = float(np.float32(c - hi).astype(jnp.bfloat16))
        out.append((hi, lo))
    return out


def _qkv_kernel(x_ref, mod_ref, w_ref, qT_ref, k_ref, vT_ref, *, n_heads, dh, scale, per_b, consts):
    tm = x_ref.shape[0]
    u = _modulate(x_ref[...], mod_ref).astype(BF16)
    p = _dot(u, w_ref[...])
    hq = n_heads * LANES
    t0 = (pl.program_id(0) % per_b) * tm
    rq = lax.broadcasted_iota(jnp.int32, (LANES, tm), 0)
    pos_q = lax.broadcasted_iota(jnp.int32, (LANES, tm), 1) + t0
    q_lo = (pos_q & (ALIBI_BLOCK - 1)).astype(F32)
    q_hi = pos_q.astype(F32) - q_lo
    lk = lax.broadcasted_iota(jnp.int32, (tm, LANES), 1)
    pos_k = lax.broadcasted_iota(jnp.int32, (tm, LANES), 0) + t0
    k_lo = (pos_k & (ALIBI_BLOCK - 1)).astype(F32)
    k_hi = pos_k.astype(F32) - k_lo

    def ext_q(e0, hi, lo):
        r = rq - e0
        v = jnp.where(r < 2, hi, jnp.where(r < 4, lo, jnp.where((r & 1) == 0, -q_lo, -q_hi)))
        return jnp.where((r >= 0) & (r < N_EXT), v, 0.0)

    def ext_k(e0, hi, lo):
        l = lk - e0
        v = jnp.where(l < 4, jnp.where((l & 1) == 0, k_lo, k_hi), jnp.where(l < 6, hi, lo))
        return jnp.where((l >= 0) & (l < N_EXT), v, 0.0)

    for h in range(n_heads):
        hi, lo = consts[h]
        hs = slice(h * LANES, (h + 1) * LANES)
        qT = (p[:, hs] * (scale * LOG2E)).T
        kh = p[:, hq + h * LANES:hq + (h + 1) * LANES]
        qT_ref[0, h, 0] = jnp.where(rq < dh, qT, ext_q(dh, hi, lo)).astype(BF16)
        qT_ref[0, h, 1] = jnp.where(rq >= dh, qT, ext_q(0, hi, lo)).astype(BF16)
        k_ref[0, h, 0] = jnp.where(lk < dh, kh, ext_k(dh, hi, lo)).astype(BF16)
        k_ref[0, h, 1] = jnp.where(lk >= dh, kh, ext_k(0, hi, lo)).astype(BF16)
        vT_ref[0, h] = p[:, 2 * hq + h * LANES:2 * hq + (h + 1) * LANES].T.astype(BF16)


def _qkv(x, mod, w, *, batch, seq, n_heads, dh, scale, consts, tm=512):
    n, d = x.shape
    wn = w.shape[1]
    tm = _row_tile(seq, tm)
    per_b = seq // tm
    return pl.pallas_call(
        functools.partial(_qkv_kernel, n_heads=n_heads, dh=dh, scale=scale, per_b=per_b, consts=consts),
        grid=(n // tm,),
        in_specs=[
            pl.BlockSpec((tm, d), lambda i: (i, 0)),
            pl.BlockSpec((1, 3, d), lambda i: (i // per_b, 0, 0)),
            _resident((d, wn), lambda i: (0, 0)),
        ],
        out_specs=[
            pl.BlockSpec((1, n_heads, 2, LANES, tm), lambda i: (i // per_b, 0, 0, 0, i % per_b)),
            pl.BlockSpec((1, n_heads, 2, tm, LANES), lambda i: (i // per_b, 0, 0, i % per_b, 0)),
            pl.BlockSpec((1, n_heads, LANES, tm), lambda i: (i // per_b, 0, 0, i % per_b)),
        ],
        out_shape=[
            jax.ShapeDtypeStruct((batch, n_heads, 2, LANES, seq), BF16),
            jax.ShapeDtypeStruct((batch, n_heads, 2, seq, LANES), BF16),
            jax.ShapeDtypeStruct((batch, n_heads, LANES, seq), BF16),
        ],
        compiler_params=_cparams("arbitrary"),
        name="qkv",
    )(x, mod, w)


def _attn_kernel(c2_ref, lam_ref, g_ref, qT_ref, k_ref, vT_ref, o_ref, s_ref, *, seq, dh, lambda_init):
    h = pl.program_id(1)
    qi = pl.program_id(2)
    tq = tk = ALIBI_BLOCK
    nk = seq // tk
    lam = lam_ref[...]
    lam_full = (jnp.exp(jnp.sum(lam[0:1] * lam[1:2], axis=-1, keepdims=True))
                - jnp.exp(jnp.sum(lam[2:3] * lam[3:4], axis=-1, keepdims=True)) + lambda_init)
    row = lax.broadcasted_iota(jnp.int32, (LANES, tq), 0)
    jj = lax.broadcasted_iota(jnp.int32, (tk, tq), 0)
    ii = lax.broadcasted_iota(jnp.int32, (tk, tq), 1)
    diag_fix = jnp.minimum((ii - jj).astype(F32) * c2_ref[h], 0.0)
    diag_rows = pl.ds(pl.multiple_of(qi * tk, tk), tk)
    outs = []
    for m in range(2):
        e0 = dh if m == 0 else 0
        q_left = qT_ref[0, 0, m]
        q_right = jnp.where((row >= e0) & (row < e0 + N_EXT), -q_left, q_left)
        mx = jnp.full((1, tq), NEG_BIG, F32)
        for j in range(nk):
            q_var = jnp.where(j > qi, q_right, q_left)
            sT = _dot(k_ref[0, 0, m, j * tk:(j + 1) * tk, :], q_var)
            s_ref[j * tk:(j + 1) * tk, :] = sT
            bm = jnp.max(sT, axis=0, keepdims=True)
            mx = jnp.maximum(mx, jnp.where(j == qi, NEG_BIG, bm))
        sd = s_ref[diag_rows, :] + diag_fix
        s_ref[diag_rows, :] = sd
        mx = jnp.maximum(mx, jnp.max(sd, axis=0, keepdims=True))
        acc = jnp.zeros((LANES, tq), F32)
        den = jnp.zeros((1, tq), F32)
        for j in range(nk):
            p = jnp.exp2(s_ref[j * tk:(j + 1) * tk, :] - mx)
            den = den + jnp.sum(p, axis=0, keepdims=True)
            acc = acc + _dot(vT_ref[0, 0, :, j * tk:(j + 1) * tk], p.astype(BF16))
        outs.append(acc / den)
    oT = outs[0] - lam_full * outs[1]
    ms = jnp.mean(oT * oT, axis=0, keepdims=True)
    on = oT * lax.rsqrt(ms + LN_EPS) * g_ref[...]
    o_ref[...] = on.T.astype(o_ref.dtype)


def _diff_attention(x, mod, w_qkv, lam, subln_g, *, batch, seq, lambda_init):
    n_heads = DA_HEADS
    dh = lam.shape[1]
    assert 2 * dh == LANES and w_qkv.shape[1] == 3 * n_heads * LANES and dh >= N_EXT
    assert seq % ALIBI_BLOCK == 0 and seq // ALIBI_BLOCK <= 256
    consts = _alibi_consts(n_heads)
    qT, k, vT = _qkv(x, mod, w_qkv.astype(BF16), batch=batch, seq=seq, n_heads=n_heads, dh=dh,
                     scale=dh ** -0.5, consts=consts)
    tq = ALIBI_BLOCK
    nq = seq // tq
    c2 = jnp.asarray([2.0 * (hi + lo) for hi, lo in consts], F32)
    g_col = (subln_g.astype(F32) * (1.0 - lambda_init)).reshape(LANES, 1)
    return pl.pallas_call(
        functools.partial(_attn_kernel, seq=seq, dh=dh, lambda_init=lambda_init),
        grid=(batch, n_heads, nq),
        in_specs=[
            pl.BlockSpec(memory_space=pltpu.SMEM),
            _resident((4, dh), lambda b, h, i: (0, 0)),
            _resident((LANES, 1), lambda b, h, i: (0, 0)),
            pl.BlockSpec((1, 1, 2, LANES, tq), lambda b, h, i: (b, h, 0, 0, i)),
            pl.BlockSpec((1, 1, 2, seq, LANES), lambda b, h, i: (b, h, 0, 0, 0)),
            pl.BlockSpec((1, 1, LANES, seq), lambda b, h, i: (b, h, 0, 0)),
        ],
        out_specs=pl.BlockSpec((tq, LANES), lambda b, h, i: (b * nq + i, h)),
        out_shape=jax.ShapeDtypeStruct((batch * seq, n_heads * LANES), BF16),
        scratch_shapes=[pltpu.VMEM((seq, tq), F32)],
        compiler_params=_cparams("arbitrary", "arbitrary", "arbitrary"),
        name="diff_attn",
    )(c2, lam.astype(F32), g_col, qT, k, vT)


def kernel(x, c, ada_w, ada_b, ln_g, ln_b, ffn_w_gate, ffn_w_up, ffn_w_down, ssm_w_in, ssm_conv_w, ssm_conv_b, ssm_dt_bias, ssm_a_log, ssm_d, ssm_norm_g, ssm_w_out, attn_w_qkv, attn_lambda, attn_subln_g, attn_w_out):
    batch, seq, d = x.shape
    depth = ada_w.shape[0]
    alpha = (2 * depth) ** 0.25
    mods = _adaln(c, ada_w, ada_b)
    xf = x.reshape(batch * seq, d)
    for i in range(depth):
        ffn = functools.partial(_ffn, seq=seq, alpha=alpha)
        xf = ffn(xf, mods[i, :, 0], ffn_w_gate[i, 0].astype(BF16), ffn_w_up[i, 0].astype(BF16),
                 ffn_w_down[i, 0].astype(BF16), ln_g[i, 0], ln_b[i, 0])
        li = i // N_MIXERS
        if i % N_MIXERS == 0:
            y = _mamba2_bidir(xf, mods[i, :, 1], ssm_w_in[li], ssm_conv_w[li], ssm_conv_b[li], ssm_dt_bias[li],
                              ssm_a_log[li], ssm_d[li], ssm_norm_g[li], batch=batch, seq=seq)
            w_out = ssm_w_out[li]
        else:
            lambda_init = 0.8 - 0.6 * math.exp(-0.3 * i)
            y = _diff_attention(xf, mods[i, :, 1], attn_w_qkv[li], attn_lambda[li], attn_subln_g[li],
                                batch=batch, seq=seq, lambda_init=lambda_init)
            w_out = attn_w_out[li]
        xf = _proj_ln(y, xf, mods[i, :, 1], w_out.astype(BF16), ln_g[i, 1], ln_b[i, 1], seq=seq, alpha=alpha)
        xf = ffn(xf, mods[i, :, 2], ffn_w_gate[i, 1].astype(BF16), ffn_w_up[i, 1].astype(BF16),
                 ffn_w_down[i, 1].astype(BF16), ln_g[i, 2], ln_b[i, 2])
    return xf.reshape(batch, seq, d)
```

```python
import functools
import math

import jax
import jax.numpy as jnp
import numpy as np
from jax import lax
from jax.experimental import pallas as pl
from jax.experimental.pallas import tpu as pltpu

F32 = jnp.float32
BF16 = jnp.bfloat16
LN_EPS = 1e-5
N_SUB = 3
N_MIXERS = 2
SSM_GROUPS = 4
SSM_STATE = 128
SSM_HEADDIM = 64
SSD_CHUNK = 128
DA_HEADS = 8
LANES = 128
HALO = 8
VMEM_LIMIT = 56 * 1024 * 1024
NEG_BIG = -1e30


def _cparams(*sem):
    return pltpu.CompilerParams(dimension_semantics=sem, vmem_limit_bytes=VMEM_LIMIT)


def _resident(shape, index_map):
    return pl.BlockSpec(shape, index_map, pipeline_mode=pl.Buffered(1))


def _dot(a, b):
    return jnp.dot(a, b, preferred_element_type=F32)


def _dot_nt(a, b):
    return lax.dot_general(a, b, (((1,), (1,)), ((), ())), preferred_element_type=F32)


def _split3(a):
    hi = a.astype(BF16)
    r = a - hi.astype(F32)
    mid = r.astype(BF16)
    lo = (r - mid.astype(F32)).astype(BF16)
    return hi, mid, lo


def _silu(v):
    return v * jax.nn.sigmoid(v)


def _layer_norm(z, g, b):
    mu = jnp.mean(z, axis=-1, keepdims=True)
    zc = z - mu
    var = jnp.mean(zc * zc, axis=-1, keepdims=True)
    return zc * lax.rsqrt(var + LN_EPS) * g + b


def _modulate(x, mod_ref):
    return x * (1.0 + mod_ref[0, 1:2, :]) + mod_ref[0, 0:1, :]


def _adaln_kernel(c_ref, w_ref, b_ref, o_ref):
    cond = _silu(c_ref[...])
    c_hi, c_mid, c_lo = _split3(cond)
    w_hi, w_mid, w_lo = _split3(w_ref[0])
    acc = _dot(c_hi, w_hi) + (_dot(c_hi, w_mid) + _dot(c_mid, w_hi))
    acc = acc + (_dot(c_hi, w_lo) + _dot(c_mid, w_mid) + _dot(c_lo, w_hi))
    o_ref[0] = acc + b_ref[0]


def _adaln(c, ada_w, ada_b):
    depth, d, n = ada_w.shape
    b = c.shape[0]
    rows = -(-b // HALO) * HALO
    c_pad = jnp.zeros((rows, d), F32).at[:b].set(c)
    tn = 1024 if n % 1024 == 0 else n
    out = pl.pallas_call(
        _adaln_kernel,
        grid=(depth, n // tn),
        in_specs=[
            pl.BlockSpec((rows, d), lambda i, j: (0, 0)),
            pl.BlockSpec((1, d, tn), lambda i, j: (i, 0, j)),
            pl.BlockSpec((1, 1, tn), lambda i, j: (i, 0, j)),
        ],
        out_specs=pl.BlockSpec((1, rows, tn), lambda i, j: (i, 0, j)),
        out_shape=jax.ShapeDtypeStruct((depth, rows, n), F32),
        compiler_params=_cparams("arbitrary", "arbitrary"),
        name="adaln",
    )(c_pad, ada_w, ada_b.reshape(depth, 1, n))
    return out[:, :b].reshape(depth, b, N_SUB, 3, d)


def _ffn_kernel(x_ref, mod_ref, wg_ref, wu_ref, wd_ref, lng_ref, lnb_ref, o_ref, *, alpha, n_sub):
    rows = x_ref.shape[0] // n_sub
    for s in range(n_sub):
        rs = slice(s * rows, (s + 1) * rows)
        x = x_ref[rs, :]
        u = _modulate(x, mod_ref).astype(BF16)
        g = _dot(u, wg_ref[...])
        v = _dot(u, wu_ref[...])
        h = (_silu(g) * v).astype(BF16)
        y = _dot(h, wd_ref[...])
        z = alpha * x + (0.5 * (1.0 + mod_ref[0, 2:3, :])) * y
        o_ref[rs, :] = _layer_norm(z, lng_ref[...], lnb_ref[...])


def _row_tile(t, want):
    tm = min(want, t)
    assert t % tm == 0
    return tm


def _ffn(x, mod, wg, wu, wd, lng, lnb, *, seq, alpha, tm=512, n_sub=2):
    n, d = x.shape
    f = wg.shape[1]
    tm = _row_tile(seq, tm)
    per_b = seq // tm
    return pl.pallas_call(
        functools.partial(_ffn_kernel, alpha=alpha, n_sub=n_sub),
        grid=(n // tm,),
        in_specs=[
            pl.BlockSpec((tm, d), lambda i: (i, 0)),
            pl.BlockSpec((1, 3, d), lambda i: (i // per_b, 0, 0)),
            _resident((d, f), lambda i: (0, 0)),
            _resident((d, f), lambda i: (0, 0)),
            _resident((f, d), lambda i: (0, 0)),
            _resident((1, d), lambda i: (0, 0)),
            _resident((1, d), lambda i: (0, 0)),
        ],
        out_specs=pl.BlockSpec((tm, d), lambda i: (i, 0)),
        out_shape=jax.ShapeDtypeStruct((n, d), F32),
        compiler_params=_cparams("arbitrary"),
        name="ffn_ln",
    )(x, mod, wg, wu, wd, lng.reshape(1, d), lnb.reshape(1, d))


def _proj_ln_kernel(a_ref, x_ref, mod_ref, w_ref, lng_ref, lnb_ref, o_ref, *, alpha):
    y = _dot(a_ref[...], w_ref[...])
    z = alpha * x_ref[...] + (1.0 + mod_ref[0, 2:3, :]) * y
    o_ref[...] = _layer_norm(z, lng_ref[...], lnb_ref[...])


def _proj_ln(a, x, mod, w, lng, lnb, *, seq, alpha, tm=512):
    n, d = x.shape
    k = a.shape[1]
    tm = _row_tile(seq, tm)
    per_b = seq // tm
    return pl.pallas_call(
        functools.partial(_proj_ln_kernel, alpha=alpha),
        grid=(n // tm,),
        in_specs=[
            pl.BlockSpec((tm, k), lambda i: (i, 0)),
            pl.BlockSpec((tm, d), lambda i: (i, 0)),
            pl.BlockSpec((1, 3, d), lambda i: (i // per_b, 0, 0)),
            _resident((k, d), lambda i: (0, 0)),
            _resident((1, d), lambda i: (0, 0)),
            _resident((1, d), lambda i: (0, 0)),
        ],
        out_specs=pl.BlockSpec((tm, d), lambda i: (i, 0)),
        out_shape=jax.ShapeDtypeStruct((n, d), F32),
        compiler_params=_cparams("arbitrary"),
        name="proj_ln",
    )(a, x, mod, w, lng.reshape(1, d), lnb.reshape(1, d))


def _ssm_in_kernel(x_ref, xp_ref, xn_ref, mod_ref, w_ref, cw_ref, cb_ref, dtb_ref, z_ref, act_ref, dt_ref,
                   *, d_inner, conv_ch, per_b):
    tm = x_ref.shape[0]
    i = pl.program_id(0)
    first = (i % per_b) == 0
    last = (i % per_b) == per_b - 1
    u_main = _modulate(x_ref[...], mod_ref)
    u_prev = jnp.where(first, 0.0, _modulate(xp_ref[...], mod_ref))
    u_next = jnp.where(last, 0.0, _modulate(xn_ref[...], mod_ref))
    u = u_main.astype(BF16)
    u_halo = jnp.concatenate([u_prev, u_main, u_next], axis=0).astype(BF16)
    rows = tm + 2 * HALO
    cblk = 512

    def conv_chunk(c0):
        cs_ = slice(c0, c0 + cblk)
        p = _dot(u_halo, w_ref[:, d_inner + c0:d_inner + c0 + cblk])
        p_prev = pltpu.roll(p, 1, axis=0)[HALO:HALO + tm]
        p_next = pltpu.roll(p, rows - 1, axis=0)[HALO:HALO + tm]
        conv = (p_prev * cw_ref[0:1, cs_] + p[HALO:HALO + tm] * cw_ref[1:2, cs_]
                + p_next * cw_ref[2:3, cs_] + cb_ref[0:1, cs_])
        act_ref[:, cs_] = _silu(conv)

    def z_chunk(c0):
        z_ref[:, c0:c0 + cblk] = _dot(u, w_ref[:, c0:c0 + cblk])

    conv_starts = list(range(0, conv_ch, cblk))
    z_starts = list(range(0, d_inner, cblk))
    while conv_starts or z_starts:
        if conv_starts:
            conv_chunk(conv_starts.pop(0))
        if z_starts:
            z_chunk(z_starts.pop(0))
    x_dt = _dot(u, w_ref[:, d_inner + conv_ch:]) + dtb_ref[...]
    dt_ref[...] = jnp.maximum(x_dt, 0.0) + jnp.log1p(jnp.exp(-jnp.abs(x_dt)))


def _ssm_in(x, mod, w_pad, conv_w, conv_b, dt_bias_l, *, seq, d_inner, conv_ch, tm=512):
    n, d = x.shape
    wn = w_pad.shape[1]
    tm = _row_tile(seq, tm)
    per_b = seq // tm
    hb = tm // HALO
    n_hb = n // HALO
    assert conv_ch % 512 == 0 and d_inner % 512 == 0
    return pl.pallas_call(
        functools.partial(_ssm_in_kernel, d_inner=d_inner, conv_ch=conv_ch, per_b=per_b),
        grid=(n // tm,),
        in_specs=[
            pl.BlockSpec((tm, d), lambda i: (i, 0)),
            pl.BlockSpec((HALO, d), lambda i: (jnp.maximum(i * hb - 1, 0), 0)),
            pl.BlockSpec((HALO, d), lambda i: (jnp.minimum((i + 1) * hb, n_hb - 1), 0)),
            pl.BlockSpec((1, 3, d), lambda i: (i // per_b, 0, 0)),
            _resident((d, wn), lambda i: (0, 0)),
            _resident((3, conv_ch), lambda i: (0, 0)),
            _resident((1, conv_ch), lambda i: (0, 0)),
            _resident((1, LANES), lambda i: (0, 0)),
        ],
        out_specs=[
            pl.BlockSpec((tm, d_inner), lambda i: (i, 0)),
            pl.BlockSpec((tm, conv_ch), lambda i: (i, 0)),
            pl.BlockSpec((tm, LANES), lambda i: (i, 0)),
        ],
        out_shape=[
            jax.ShapeDtypeStruct((n, d_inner), F32),
            jax.ShapeDtypeStruct((n, conv_ch), F32),
            jax.ShapeDtypeStruct((n, LANES), F32),
        ],
        compiler_params=_cparams("arbitrary"),
        name="ssm_in",
    )(x, x, x, mod, w_pad, conv_w, conv_b.reshape(1, conv_ch), dt_bias_l)


def _ssd_kernel(*refs, reverse, final, n_heads):
    if final:
        act_ref, dt_ref, alog_ref, z_ref, yb_ref, dsk_ref, ng_ref, out_ref, state_ref, y_ref = refs
    else:
        act_ref, dt_ref, alog_ref, out_ref, state_ref = refs
    L = SSD_CHUNK
    P = SSM_HEADDIM
    G = SSM_GROUPS
    NS = SSM_STATE
    H = n_heads
    HPG = H // G
    d_inner = H * P
    assert L == 2 * P == LANES and HPG % 2 == 0

    @pl.when(pl.program_id(1) == 0)
    def _():
        state_ref[...] = jnp.zeros_like(state_ref)

    lane0 = H if reverse else 0
    dt = dt_ref[0]
    a = dt * (-jnp.exp(alog_ref[...]))
    ri = lax.broadcasted_iota(jnp.int32, (L, L), 0)
    ci = lax.broadcasted_iota(jnp.int32, (L, L), 1)
    mask = (ci >= ri) if reverse else (ci <= ri)
    tri = jnp.where(mask, 1.0, 0.0).astype(BF16)
    a_hi, a_mid, a_lo = _split3(a)
    cs = _dot(tri, a_hi) + (_dot(tri, a_mid) + _dot(tri, a_lo))
    csT = cs.T
    dtT = dt.T
    r_tot = 0 if reverse else L - 1
    wT = jnp.exp(csT[:, r_tot:r_tot + 1] - csT) * dtT
    csT_dt = csT - jnp.log(dtT)
    lane = lax.broadcasted_iota(jnp.int32, (L, LANES), 1)
    lo_half = lane < P

    for g in range(G):
        b0 = d_inner + g * NS
        c0 = d_inner + G * NS + g * NS
        Bg = act_ref[0, :, b0:b0 + NS]
        Cg = act_ref[0, :, c0:c0 + NS].astype(BF16)
        CB = _dot_nt(Cg, Bg.astype(BF16))
        BgT = Bg.T
        yoff = _dot(Cg, state_ref[g].astype(BF16))
        for k in range(HPG // 2):
            h0 = g * HPG + 2 * k
            ms, bws, es = [], [], []
            for h in (h0, h0 + 1):
                hl = lane0 + h
                colb = jnp.broadcast_to(cs[:, hl:hl + 1], (L, L))
                seg = colb - csT_dt[hl:hl + 1, :]
                lm = jnp.exp(jnp.where(mask, seg, NEG_BIG))
                ms.append((CB * lm).astype(BF16))
                bws.append((BgT * wT[hl:hl + 1, :]).astype(BF16))
                es.append(jnp.exp(colb))
            lhs = jnp.concatenate(
                [jnp.concatenate(ms, axis=1), jnp.concatenate(bws, axis=1)], axis=0)
            xp = act_ref[0, :, h0 * P:(h0 + 2) * P]
            rhs = jnp.concatenate(
                [jnp.where(lo_half, xp, 0.0), jnp.where(lo_half, 0.0, xp)], axis=0).astype(BF16)
            res = _dot(lhs, rhs)
            e_pair = jnp.where(lo_half, es[0], es[1])
            lsl = slice(2 * k * P, (2 * k + 2) * P)
            y_pair = res[:L] + e_pair * yoff[:, lsl]
            state_ref[g, :, lsl] = state_ref[g, :, lsl] * e_pair[r_tot:r_tot + 1, :] + res[L:]
            osl = slice(h0 * P, (h0 + 2) * P)
            if final:
                y_ref[:, osl] = y_pair + yb_ref[0, :, osl] + xp * dsk_ref[0:1, osl]
            else:
                out_ref[0, :, osl] = y_pair

    if final:
        gw = d_inner // G
        for g in range(G):
            gs = slice(g * gw, (g + 1) * gw)
            y = y_ref[:, gs] * _silu(z_ref[0, :, gs])
            ms_ = jnp.mean(y * y, axis=-1, keepdims=True)
            out_ref[0, :, gs] = (y * lax.rsqrt(ms_ + LN_EPS) * ng_ref[0:1, gs]).astype(out_ref.dtype)


def _ssd(act, dt, a_log_l, finish, *, reverse, n_heads):
    b, t, conv_ch = act.shape
    L = SSD_CHUNK
    nc = t // L
    d_inner = n_heads * SSM_HEADDIM

    def blk(w):
        return pl.BlockSpec((1, L, w), lambda i, c: (i, (nc - 1 - c) if reverse else c, 0))

    def const(w):
        return _resident((1, w), lambda i, c: (0, 0))

    in_specs = [blk(conv_ch), blk(LANES), const(LANES)]
    args = [act, dt, a_log_l]
    scratch = [pltpu.VMEM((SSM_GROUPS, SSM_STATE, d_inner // SSM_GROUPS), F32)]
    if finish is not None:
        z, y_other, d_skip_l, norm_g = finish
        in_specs += [blk(d_inner), blk(d_inner), const(d_inner), const(d_inner)]
        args += [z, y_other, d_skip_l, norm_g.reshape(1, d_inner)]
        scratch.append(pltpu.VMEM((L, d_inner), F32))
    return pl.pallas_call(
        functools.partial(_ssd_kernel, reverse=reverse, final=finish is not None, n_heads=n_heads),
        grid=(b, nc),
        in_specs=in_specs,
        out_specs=blk(d_inner),
        out_shape=jax.ShapeDtypeStruct((b, t, d_inner), F32 if finish is None else BF16),
        scratch_shapes=scratch,
        compiler_params=_cparams("arbitrary", "arbitrary"),
        name="ssd_bwd" if reverse else "ssd_fwd_final",
    )(*args)


def _lane_pack(v2, n_heads):
    flat = v2.reshape(1, 2 * n_heads).astype(F32)
    return jnp.zeros((1, LANES), F32).at[:, :2 * n_heads].set(flat)


def _mamba2_bidir(x, mod, w_in, conv_w, conv_b, dt_bias, a_log, d_skip, norm_g, *, batch, seq):
    d = x.shape[1]
    n_heads = d_skip.shape[0]
    d_inner = n_heads * SSM_HEADDIM
    conv_ch = conv_w.shape[1]
    assert 2 * n_heads <= LANES
    n_in = w_in.shape[1]
    w_pad = jnp.zeros((d, d_inner + conv_ch + LANES), BF16).at[:, :n_in].set(w_in.astype(BF16))
    dtb = _lane_pack(dt_bias, n_heads)
    alog = _lane_pack(a_log, n_heads)
    z, act, dt = _ssm_in(x, mod, w_pad, conv_w, conv_b, dtb, seq=seq, d_inner=d_inner, conv_ch=conv_ch)
    z = z.reshape(batch, seq, d_inner)
    act = act.reshape(batch, seq, conv_ch)
    dt = dt.reshape(batch, seq, LANES)
    dsk = jnp.repeat(d_skip.astype(F32), SSM_HEADDIM).reshape(1, d_inner)
    yb = _ssd(act, dt, alog, None, reverse=True, n_heads=n_heads)
    y = _ssd(act, dt, alog, (z, yb, dsk, norm_g), reverse=False, n_heads=n_heads)
    return y.reshape(batch * seq, d_inner)


ALIBI_BLOCK = 256
N_EXT = 8
DEN_ROWS = 16
LOG2E = math.log2(math.e)


def _alibi_consts(n_heads):
    out = []
    for h in range(n_heads):
        c = 2.0 ** (-8.0 * (h + 1) / n_heads) * LOG2E
        hi = float(np.float32(c).astype(jnp.bfloat16))
        lo = float(np.float32(c - hi).astype(jnp.bfloat16))
        out.append((hi, lo))
    return out


def _qkv_kernel(x_ref, mod_ref, w_ref, qT_ref, k_ref, vT_ref, *, n_heads, dh, scale, per_b, consts):
    tm = x_ref.shape[0]
    u = _modulate(x_ref[...], mod_ref).astype(BF16)
    p = _dot(u, w_ref[...])
    hq = n_heads * LANES
    t0 = (pl.program_id(0) % per_b) * tm
    rq = lax.broadcasted_iota(jnp.int32, (LANES, tm), 0)
    pos_q = lax.broadcasted_iota(jnp.int32, (LANES, tm), 1) + t0
    q_lo = (pos_q & (ALIBI_BLOCK - 1)).astype(F32)
    q_hi = pos_q.astype(F32) - q_lo
    lk = lax.broadcasted_iota(jnp.int32, (tm, LANES), 1)
    pos_k = lax.broadcasted_iota(jnp.int32, (tm, LANES), 0) + t0
    k_lo = (pos_k & (ALIBI_BLOCK - 1)).astype(F32)
    k_hi = pos_k.astype(F32) - k_lo

    def ext_q(e0, hi, lo):
        r = rq - e0
        v = jnp.where(r < 2, hi, jnp.where(r < 4, lo, jnp.where((r & 1) == 0, -q_lo, -q_hi)))
        return jnp.where((r >= 0) & (r < N_EXT), v, 0.0)

    def ext_k(e0, hi, lo):
        l = lk - e0
        v = jnp.where(l < 4, jnp.where((l & 1) == 0, k_lo, k_hi), jnp.where(l < 6, hi, lo))
        return jnp.where((l >= 0) & (l < N_EXT), v, 0.0)

    for h in range(n_heads):
        hi, lo = consts[h]
        hs = slice(h * LANES, (h + 1) * LANES)
        qT = (p[:, hs] * (scale * LOG2E)).T
        kh = p[:, hq + h * LANES:hq + (h + 1) * LANES]
        qT_ref[0, h, 0] = jnp.where(rq < dh, qT, ext_q(dh, hi, lo)).astype(BF16)
        qT_ref[0, h, 1] = jnp.where(rq >= dh, qT, ext_q(0, hi, lo)).astype(BF16)
        k_ref[0, h, 0] = jnp.where(lk < dh, kh, ext_k(dh, hi, lo)).astype(BF16)
        k_ref[0, h, 1] = jnp.where(lk >= dh, kh, ext_k(0, hi, lo)).astype(BF16)
        vT_ref[0, h, 0:LANES] = p[:, 2 * hq + h * LANES:2 * hq + (h + 1) * LANES].T.astype(BF16)
        vT_ref[0, h, LANES:LANES + DEN_ROWS] = jnp.ones((DEN_ROWS, tm), BF16)


def _qkv(x, mod, w, *, batch, seq, n_heads, dh, scale, consts, tm=512):
    n, d = x.shape
    wn = w.shape[1]
    tm = _row_tile(seq, tm)
    per_b = seq // tm
    return pl.pallas_call(
        functools.partial(_qkv_kernel, n_heads=n_heads, dh=dh, scale=scale, per_b=per_b, consts=consts),
        grid=(n // tm,),
        in_specs=[
            pl.BlockSpec((tm, d), lambda i: (i, 0)),
            pl.BlockSpec((1, 3, d), lambda i: (i // per_b, 0, 0)),
            _resident((d, wn), lambda i: (0, 0)),
        ],
        out_specs=[
            pl.BlockSpec((1, n_heads, 2, LANES, tm), lambda i: (i // per_b, 0, 0, 0, i % per_b)),
            pl.BlockSpec((1, n_heads, 2, tm, LANES), lambda i: (i // per_b, 0, 0, i % per_b, 0)),
            pl.BlockSpec((1, n_heads, LANES + DEN_ROWS, tm), lambda i: (i // per_b, 0, 0, i % per_b)),
        ],
        out_shape=[
            jax.ShapeDtypeStruct((batch, n_heads, 2, LANES, seq), BF16),
            jax.ShapeDtypeStruct((batch, n_heads, 2, seq, LANES), BF16),
            jax.ShapeDtypeStruct((batch, n_heads, LANES + DEN_ROWS, seq), BF16),
        ],
        compiler_params=_cparams("arbitrary"),
        name="qkv",
    )(x, mod, w)


def _attn_kernel(c2_ref, lam_ref, g_ref, qp_ref, qn_ref, kc_ref, kn_ref, v_ref, o_ref,
                 sx_ref, sy_ref, mxx_ref, *, seq, dh, n_heads, n_blocks, lambda_init):
    g = pl.program_id(0)
    tb = ALIBI_BLOCK
    nk = seq // tb
    steps_per_bh = nk // 2
    h_cur = (g // steps_per_bh) % n_heads
    blk_even = (g % steps_per_bh) * 2
    nxt = jnp.minimum(2 * g + 2, n_blocks - 1)
    h_next = (nxt // nk) % n_heads
    blk_next = nxt % nk

    lam = lam_ref[...]
    lam_full = (jnp.exp(jnp.sum(lam[0:1] * lam[1:2], axis=-1, keepdims=True))
                - jnp.exp(jnp.sum(lam[2:3] * lam[3:4], axis=-1, keepdims=True)) + lambda_init)
    row = lax.broadcasted_iota(jnp.int32, (LANES, tb), 0)
    jj = lax.broadcasted_iota(jnp.int32, (tb, tb), 0)
    ii = lax.broadcasted_iota(jnp.int32, (tb, tb), 1)
    dist = (ii - jj).astype(F32)

    def scores_phase(q_maps, blk, head, k_ref, s_ref):
        q_vars = []
        for m in range(2):
            e0 = dh if m == 0 else 0
            q_left = q_maps[m]
            q_right = jnp.where((row >= e0) & (row < e0 + N_EXT), -q_left, q_left)
            q_vars.append((q_left, q_right))
        mx = [jnp.full((1, tb), NEG_BIG, F32), jnp.full((1, tb), NEG_BIG, F32)]

        def block(j):
            for m in range(2):
                q_var = jnp.where(j > blk, q_vars[m][1], q_vars[m][0])
                sT = _dot(k_ref[0, m, j * tb:(j + 1) * tb, :], q_var)
                s_ref[m, j * tb:(j + 1) * tb, :] = sT
                bm = jnp.max(sT, axis=0, keepdims=True)
                mx[m] = jnp.maximum(mx[m], jnp.where(j == blk, NEG_BIG, bm))

        def finish():
            fix = jnp.minimum(dist * c2_ref[head], 0.0)
            rows = pl.ds(pl.multiple_of(blk * tb, tb), tb)
            for m in range(2):
                sd = s_ref[m, rows, :] + fix
                s_ref[m, rows, :] = sd
                mx[m] = jnp.maximum(mx[m], jnp.max(sd, axis=0, keepdims=True))
            return mx

        return block, finish

    def values_phase(s_ref, mx, out_rows):
        acc = [jnp.zeros((LANES + DEN_ROWS, tb), F32), jnp.zeros((LANES + DEN_ROWS, tb), F32)]

        def block(j):
            for m in range(2):
                p = jnp.exp2(s_ref[m, j * tb:(j + 1) * tb, :] - mx[m])
                acc[m] = acc[m] + _dot(v_ref[0, :, j * tb:(j + 1) * tb], p.astype(BF16))

        def finish():
            outs = [a[:LANES] * (1.0 / a[LANES:LANES + 1]) for a in acc]
            oT = outs[0] - lam_full * outs[1]
            ms = jnp.mean(oT * oT, axis=0, keepdims=True)
            on = oT * lax.rsqrt(ms + LN_EPS) * g_ref[...]
            o_ref[out_rows, :] = on.T.astype(o_ref.dtype)

        return block, finish

    def run(phase_a, phase_b):
        for j in range(nk):
            phase_a[0](j)
            phase_b[0](j)
        res = phase_a[1]()
        phase_b[1]()
        return res

    @pl.when(g == 0)
    def _():
        blk, fin = scores_phase([qp_ref[0, 0, :, 0:tb], qp_ref[0, 1, :, 0:tb]], blk_even, h_cur, kc_ref, sx_ref)
        for j in range(nk):
            blk(j)
        mx0 = fin()
        mxx_ref[0] = mx0[0]
        mxx_ref[1] = mx0[1]

    mx_even = [mxx_ref[0], mxx_ref[1]]
    mx_odd = run(
        scores_phase([qp_ref[0, 0, :, tb:2 * tb], qp_ref[0, 1, :, tb:2 * tb]], blk_even + 1, h_cur, kc_ref, sy_ref),
        values_phase(sx_ref, mx_even, slice(0, tb)))
    mx_next = run(
        scores_phase([qn_ref[0, 0], qn_ref[0, 1]], blk_next, h_next, kn_ref, sx_ref),
        values_phase(sy_ref, mx_odd, slice(tb, 2 * tb)))
    mxx_ref[0] = mx_next[0]
    mxx_ref[1] = mx_next[1]


def _diff_attention(x, mod, w_qkv, lam, subln_g, *, batch, seq, lambda_init):
    n_heads = DA_HEADS
    dh = lam.shape[1]
    assert 2 * dh == LANES and w_qkv.shape[1] == 3 * n_heads * LANES and dh >= N_EXT
    tb = ALIBI_BLOCK
    nk = seq // tb
    assert seq % (2 * tb) == 0 and nk <= 256
    consts = _alibi_consts(n_heads)
    qT, k, vT = _qkv(x, mod, w_qkv.astype(BF16), batch=batch, seq=seq, n_heads=n_heads, dh=dh,
                     scale=dh ** -0.5, consts=consts)
    n_bh = batch * n_heads
    qT = qT.reshape(n_bh, 2, LANES, seq)
    k = k.reshape(n_bh, 2, seq, LANES)
    vT = vT.reshape(n_bh, LANES + DEN_ROWS, seq)
    spb = nk // 2
    n_blocks = n_bh * nk

    def nxt(g):
        return jnp.minimum(2 * g + 2, n_blocks - 1)

    c2 = jnp.asarray([2.0 * (hi + lo) for hi, lo in consts], F32)
    g_col = (subln_g.astype(F32) * (1.0 - lambda_init)).reshape(LANES, 1)
    return pl.pallas_call(
        functools.partial(_attn_kernel, seq=seq, dh=dh, n_heads=n_heads, n_blocks=n_blocks, lambda_init=lambda_init),
        grid=(n_bh * spb,),
        in_specs=[
            pl.BlockSpec(memory_space=pltpu.SMEM),
            _resident((4, dh), lambda g: (0, 0)),
            _resident((LANES, 1), lambda g: (0, 0)),
            pl.BlockSpec((1, 2, LANES, 2 * tb), lambda g: (g // spb, 0, 0, g % spb)),
            pl.BlockSpec((1, 2, LANES, tb), lambda g: (nxt(g) // nk, 0, 0, nxt(g) % nk)),
            pl.BlockSpec((1, 2, seq, LANES), lambda g: (g // spb, 0, 0, 0)),
            pl.BlockSpec((1, 2, seq, LANES), lambda g: (nxt(g) // nk, 0, 0, 0)),
            pl.BlockSpec((1, LANES + DEN_ROWS, seq), lambda g: (g // spb, 0, 0)),
        ],
        out_specs=pl.BlockSpec(
            (2 * tb, LANES), lambda g: ((g // spb // n_heads) * spb + g % spb, (g // spb) % n_heads)),
        out_shape=jax.ShapeDtypeStruct((batch * seq, n_heads * LANES), BF16),
        scratch_shapes=[
            pltpu.VMEM((2, seq, tb), F32),
            pltpu.VMEM((2, seq, tb), F32),
            pltpu.VMEM((2, 1, tb), F32),
        ],
        compiler_params=_cparams("arbitrary"),
        name="diff_attn",
    )(c2, lam.astype(F32), g_col, qT, qT, k, k, vT)


def kernel(x, c, ada_w, ada_b, ln_g, ln_b, ffn_w_gate, ffn_w_up, ffn_w_down, ssm_w_in, ssm_conv_w, ssm_conv_b, ssm_dt_bias, ssm_a_log, ssm_d, ssm_norm_g, ssm_w_out, attn_w_qkv, attn_lambda, attn_subln_g, attn_w_out):
    batch, seq, d = x.shape
    depth = ada_w.shape[0]
    alpha = (2 * depth) ** 0.25
    mods = _adaln(c, ada_w, ada_b)
    xf = x.reshape(batch * seq, d)
    for i in range(depth):
        ffn = functools.partial(_ffn, seq=seq, alpha=alpha)
        xf = ffn(xf, mods[i, :, 0], ffn_w_gate[i, 0].astype(BF16), ffn_w_up[i, 0].astype(BF16),
                 ffn_w_down[i, 0].astype(BF16), ln_g[i, 0], ln_b[i, 0])
        li = i // N_MIXERS
        if i % N_MIXERS == 0:
            y = _mamba2_bidir(xf, mods[i, :, 1], ssm_w_in[li], ssm_conv_w[li], ssm_conv_b[li], ssm_dt_bias[li],
                              ssm_a_log[li], ssm_d[li], ssm_norm_g[li], batch=batch, seq=seq)
            w_out = ssm_w_out[li]
        else:
            lambda_init = 0.8 - 0.6 * math.exp(-0.3 * i)
            y = _diff_attention(xf, mods[i, :, 1], attn_w_qkv[li], attn_lambda[li], attn_subln_g[li],
                                batch=batch, seq=seq, lambda_init=lambda_init)
            w_out = attn_w_out[li]
        xf = _proj_ln(y, xf, mods[i, :, 1], w_out.astype(BF16), ln_g[i, 1], ln_b[i, 1], seq=seq, alpha=alpha)
        xf = ffn(xf, mods[i, :, 2], ffn_w_gate[i, 1].astype(BF16), ffn_w_up[i, 1].astype(BF16),
                 ffn_w_down[i, 1].astype(BF16), ln_g[i, 2], ln_b[i, 2])
    return xf.reshape(batch, seq, d)
```

```python
import functools
import math

import jax
import jax.numpy as jnp
import numpy as np
from jax import lax
from jax.experimental import pallas as pl
from jax.experimental.pallas import tpu as pltpu

F32 = jnp.float32
BF16 = jnp.bfloat16
LN_EPS = 1e-5
N_SUB = 3
N_MIXERS = 2
SSM_GROUPS = 4
SSM_STATE = 128
SSM_HEADDIM = 64
SSD_CHUNK = 128
DA_HEADS = 8
LANES = 128
HALO = 8
VMEM_LIMIT = 56 * 1024 * 1024
NEG_BIG = -1e30


def _cparams(*sem):
    return pltpu.CompilerParams(dimension_semantics=sem, vmem_limit_bytes=VMEM_LIMIT)


def _resident(shape, index_map):
    return pl.BlockSpec(shape, index_map, pipeline_mode=pl.Buffered(1))


def _dot(a, b):
    return jnp.dot(a, b, preferred_element_type=F32)


def _dot_nt(a, b):
    return lax.dot_general(a, b, (((1,), (1,)), ((), ())), preferred_element_type=F32)


def _split3(a):
    hi = a.astype(BF16)
    r = a - hi.astype(F32)
    mid = r.astype(BF16)
    lo = (r - mid.astype(F32)).astype(BF16)
    return hi, mid, lo


def _silu(v):
    return v * jax.nn.sigmoid(v)


def _layer_norm(z, g, b):
    mu = jnp.mean(z, axis=-1, keepdims=True)
    zc = z - mu
    var = jnp.mean(zc * zc, axis=-1, keepdims=True)
    return zc * lax.rsqrt(var + LN_EPS) * g + b


def _modulate(x, mod_ref):
    return x * (1.0 + mod_ref[0, 1:2, :]) + mod_ref[0, 0:1, :]


def _adaln_kernel(c_ref, w_ref, b_ref, o_ref):
    cond = _silu(c_ref[...])
    c_hi, c_lo, _ = _split3(cond)
    w_hi, w_lo, _ = _split3(w_ref[0])
    acc = _dot(c_hi, w_hi) + (_dot(c_hi, w_lo) + _dot(c_lo, w_hi))
    o_ref[0] = acc + b_ref[0]


def _adaln(c, ada_w, ada_b):
    depth, d, n = ada_w.shape
    b = c.shape[0]
    rows = -(-b // HALO) * HALO
    c_pad = jnp.zeros((rows, d), F32).at[:b].set(c)
    tn = 1024 if n % 1024 == 0 else n
    out = pl.pallas_call(
        _adaln_kernel,
        grid=(depth, n // tn),
        in_specs=[
            pl.BlockSpec((rows, d), lambda i, j: (0, 0)),
            pl.BlockSpec((1, d, tn), lambda i, j: (i, 0, j)),
            pl.BlockSpec((1, 1, tn), lambda i, j: (i, 0, j)),
        ],
        out_specs=pl.BlockSpec((1, rows, tn), lambda i, j: (i, 0, j)),
        out_shape=jax.ShapeDtypeStruct((depth, rows, n), F32),
        compiler_params=_cparams("arbitrary", "arbitrary"),
        name="adaln",
    )(c_pad, ada_w, ada_b.reshape(depth, 1, n))
    return out[:, :b].reshape(depth, b, N_SUB, 3, d)


def _ffn_kernel(x_ref, mod_ref, wg_ref, wu_ref, wd_ref, lng_ref, lnb_ref, o_ref, *, alpha, n_sub):
    rows = x_ref.shape[0] // n_sub
    for s in range(n_sub):
        rs = slice(s * rows, (s + 1) * rows)
        x = x_ref[rs, :]
        u = _modulate(x, mod_ref).astype(BF16)
        g = _dot(u, wg_ref[...])
        v = _dot(u, wu_ref[...])
        h = (_silu(g) * v).astype(BF16)
        y = _dot(h, wd_ref[...])
        z = alpha * x + (0.5 * (1.0 + mod_ref[0, 2:3, :])) * y
        o_ref[rs, :] = _layer_norm(z, lng_ref[...], lnb_ref[...])


def _row_tile(t, want):
    tm = min(want, t)
    assert t % tm == 0
    return tm


def _ffn(x, mod, wg, wu, wd, lng, lnb, *, layer, sub, seq, alpha, tm=1024, sub_rows=256):
    n, d = x.shape
    f = wg.shape[-1]
    tm = _row_tile(seq, tm)
    n_sub = max(tm // sub_rows, 1)
    per_b = seq // tm
    return pl.pallas_call(
        functools.partial(_ffn_kernel, alpha=alpha, n_sub=n_sub),
        grid=(n // tm,),
        in_specs=[
            pl.BlockSpec((tm, d), lambda i: (i, 0)),
            pl.BlockSpec((1, 3, d), lambda i: (i // per_b, 0, 0)),
            _resident((None, None, d, f), lambda i: (layer, sub, 0, 0)),
            _resident((None, None, d, f), lambda i: (layer, sub, 0, 0)),
            _resident((None, None, f, d), lambda i: (layer, sub, 0, 0)),
            _resident((1, d), lambda i: (0, 0)),
            _resident((1, d), lambda i: (0, 0)),
        ],
        out_specs=pl.BlockSpec((tm, d), lambda i: (i, 0)),
        out_shape=jax.ShapeDtypeStruct((n, d), F32),
        compiler_params=_cparams("arbitrary"),
        name="ffn_ln",
    )(x, mod, wg, wu, wd, lng.reshape(1, d), lnb.reshape(1, d))


def _proj_ln_kernel(a_ref, x_ref, mod_ref, w_ref, lng_ref, lnb_ref, o_ref, *, alpha, n_sub):
    rows = x_ref.shape[0] // n_sub
    for s in range(n_sub):
        rs = slice(s * rows, (s + 1) * rows)
        y = _dot(a_ref[rs, :], w_ref[...])
        z = alpha * x_ref[rs, :] + (1.0 + mod_ref[0, 2:3, :]) * y
        o_ref[rs, :] = _layer_norm(z, lng_ref[...], lnb_ref[...])


def _proj_ln(a, x, mod, w, lng, lnb, *, seq, alpha, tm=1024, sub_rows=256):
    n, d = x.shape
    k = a.shape[1]
    tm = _row_tile(seq, tm)
    n_sub = max(tm // sub_rows, 1)
    per_b = seq // tm
    return pl.pallas_call(
        functools.partial(_proj_ln_kernel, alpha=alpha, n_sub=n_sub),
        grid=(n // tm,),
        in_specs=[
            pl.BlockSpec((tm, k), lambda i: (i, 0)),
            pl.BlockSpec((tm, d), lambda i: (i, 0)),
            pl.BlockSpec((1, 3, d), lambda i: (i // per_b, 0, 0)),
            _resident((k, d), lambda i: (0, 0)),
            _resident((1, d), lambda i: (0, 0)),
            _resident((1, d), lambda i: (0, 0)),
        ],
        out_specs=pl.BlockSpec((tm, d), lambda i: (i, 0)),
        out_shape=jax.ShapeDtypeStruct((n, d), F32),
        compiler_params=_cparams("arbitrary"),
        name="proj_ln",
    )(a, x, mod, w, lng.reshape(1, d), lnb.reshape(1, d))


def _ssm_in_kernel(x_ref, xp_ref, xn_ref, mod_ref, w_ref, wdt_ref, cw_ref, cb_ref, dtb_ref, alog_ref,
                   z_ref, act_ref, cs_ref, ctd_ref, wT_ref, *, d_inner, conv_ch, per_b, n_heads):
    tm = x_ref.shape[0]
    i = pl.program_id(0)
    first = (i % per_b) == 0
    last = (i % per_b) == per_b - 1
    u_main = _modulate(x_ref[...], mod_ref)
    u_prev = jnp.where(first, 0.0, _modulate(xp_ref[...], mod_ref))
    u_next = jnp.where(last, 0.0, _modulate(xn_ref[...], mod_ref))
    u = u_main.astype(BF16)
    u_halo = jnp.concatenate([u_prev, u_main, u_next], axis=0).astype(BF16)
    rows = tm + 2 * HALO
    cblk = 512

    def conv_chunk(c0):
        cs_ = slice(c0, c0 + cblk)
        p = _dot(u_halo, w_ref[:, d_inner + c0:d_inner + c0 + cblk])
        p_prev = pltpu.roll(p, 1, axis=0)[HALO:HALO + tm]
        p_next = pltpu.roll(p, rows - 1, axis=0)[HALO:HALO + tm]
        conv = (p_prev * cw_ref[0:1, cs_] + p[HALO:HALO + tm] * cw_ref[1:2, cs_]
                + p_next * cw_ref[2:3, cs_] + cb_ref[0:1, cs_])
        act_ref[:, cs_] = _silu(conv)

    def z_chunk(c0):
        z_ref[:, c0:c0 + cblk] = _dot(u, w_ref[:, c0:c0 + cblk])

    L = SSD_CHUNK
    x_dt = _dot(u, wdt_ref[...]) + dtb_ref[...]
    dt = jnp.maximum(x_dt, 0.0) + jnp.log1p(jnp.exp(-jnp.abs(x_dt)))
    a = dt * (-jnp.exp(alog_ref[...]))
    ri = lax.broadcasted_iota(jnp.int32, (L, L), 0)
    ci = lax.broadcasted_iota(jnp.int32, (L, L), 1)
    tri = jnp.where(ci <= ri, 1.0, 0.0).astype(BF16)
    fwd_lane = lax.broadcasted_iota(jnp.int32, (L, LANES), 1) < n_heads
    fwd_row = lax.broadcasted_iota(jnp.int32, (LANES, L), 0) < n_heads
    for k in range(tm // L):
        rs = slice(k * L, (k + 1) * L)
        a_k = a[rs]
        a_hi, a_mid, a_lo = _split3(a_k)
        cs_f = _dot(tri, a_hi) + (_dot(tri, a_mid) + _dot(tri, a_lo))
        cs_b = cs_f[L - 1:L, :] - cs_f + a_k
        cs = jnp.where(fwd_lane, cs_f, cs_b)
        csT = cs.T
        dtT = dt[rs].T
        totT = jnp.where(fwd_row, csT[:, L - 1:L], csT[:, 0:1])
        cs_ref[rs, :] = cs
        ctd_ref[k] = csT - jnp.log(dtT)
        wT_ref[k] = jnp.exp(totT - csT) * dtT

    conv_starts = list(range(0, conv_ch, cblk))
    z_starts = list(range(0, d_inner, cblk))
    while conv_starts or z_starts:
        if conv_starts:
            conv_chunk(conv_starts.pop(0))
        if z_starts:
            z_chunk(z_starts.pop(0))


def _ssm_in(x, mod, w_main, w_dt, conv_w, conv_b, dt_bias_l, a_log_l, *, seq, d_inner, conv_ch, n_heads, tm=512):
    n, d = x.shape
    tm = _row_tile(seq, tm)
    per_b = seq // tm
    hb = tm // HALO
    n_hb = n // HALO
    L = SSD_CHUNK
    assert conv_ch % 512 == 0 and d_inner % 512 == 0 and tm % L == 0
    return pl.pallas_call(
        functools.partial(_ssm_in_kernel, d_inner=d_inner, conv_ch=conv_ch, per_b=per_b, n_heads=n_heads),
        grid=(n // tm,),
        in_specs=[
            pl.BlockSpec((tm, d), lambda i: (i, 0)),
            pl.BlockSpec((HALO, d), lambda i: (jnp.maximum(i * hb - 1, 0), 0)),
            pl.BlockSpec((HALO, d), lambda i: (jnp.minimum((i + 1) * hb, n_hb - 1), 0)),
            pl.BlockSpec((1, 3, d), lambda i: (i // per_b, 0, 0)),
            _resident((d, d_inner + conv_ch), lambda i: (0, 0)),
            _resident((d, LANES), lambda i: (0, 0)),
            _resident((3, conv_ch), lambda i: (0, 0)),
            _resident((1, conv_ch), lambda i: (0, 0)),
            _resident((1, LANES), lambda i: (0, 0)),
            _resident((1, LANES), lambda i: (0, 0)),
        ],
        out_specs=[
            pl.BlockSpec((tm, d_inner), lambda i: (i, 0)),
            pl.BlockSpec((tm, conv_ch), lambda i: (i, 0)),
            pl.BlockSpec((tm, LANES), lambda i: (i, 0)),
            pl.BlockSpec((tm // L, LANES, L), lambda i: (i, 0, 0)),
            pl.BlockSpec((tm // L, LANES, L), lambda i: (i, 0, 0)),
        ],
        out_shape=[
            jax.ShapeDtypeStruct((n, d_inner), F32),
            jax.ShapeDtypeStruct((n, conv_ch), F32),
            jax.ShapeDtypeStruct((n, LANES), F32),
            jax.ShapeDtypeStruct((n // L, LANES, L), F32),
            jax.ShapeDtypeStruct((n // L, LANES, L), F32),
        ],
        compiler_params=_cparams("arbitrary"),
        name="ssm_in",
    )(x, x, x, mod, w_main, w_dt, conv_w, conv_b.reshape(1, conv_ch), dt_bias_l, a_log_l)


def _ssd_kernel(*refs, reverse, final, n_heads):
    if final:
        act_ref, cs_ref, ctd_ref, wT_ref, z_ref, yb_ref, dsk_ref, ng_ref, out_ref, state_ref, y_ref = refs
    else:
        act_ref, cs_ref, ctd_ref, wT_ref, out_ref, state_ref = refs
    L = SSD_CHUNK
    P = SSM_HEADDIM
    G = SSM_GROUPS
    NS = SSM_STATE
    H = n_heads
    HPG = H // G
    d_inner = H * P
    assert L == 2 * P == LANES and HPG % 2 == 0

    @pl.when(pl.program_id(1) == 0)
    def _():
        state_ref[...] = jnp.zeros_like(state_ref)

    lane0 = H if reverse else 0
    cs = cs_ref[0]
    csT_dt = ctd_ref[0, 0]
    wT = wT_ref[0, 0]
    ri = lax.broadcasted_iota(jnp.int32, (L, L), 0)
    ci = lax.broadcasted_iota(jnp.int32, (L, L), 1)
    mask = (ci >= ri) if reverse else (ci <= ri)
    r_tot = 0 if reverse else L - 1
    lane = lax.broadcasted_iota(jnp.int32, (L, LANES), 1)
    lo_half = lane < P

    for g in range(G):
        b0 = d_inner + g * NS
        c0 = d_inner + G * NS + g * NS
        Bg = act_ref[0, :, b0:b0 + NS]
        Cg = act_ref[0, :, c0:c0 + NS].astype(BF16)
        CB = _dot_nt(Cg, Bg.astype(BF16))
        BgT = Bg.T
        yoff = _dot(Cg, state_ref[g].astype(BF16))
        for k in range(HPG // 2):
            h0 = g * HPG + 2 * k
            ms, bws, es = [], [], []
            for h in (h0, h0 + 1):
                hl = lane0 + h
                colb = jnp.broadcast_to(cs[:, hl:hl + 1], (L, L))
                seg = colb - csT_dt[hl:hl + 1, :]
                lm = jnp.exp(jnp.where(mask, seg, NEG_BIG))
                ms.append((CB * lm).astype(BF16))
                bws.append((BgT * wT[hl:hl + 1, :]).astype(BF16))
                es.append(jnp.exp(colb))
            lhs = jnp.concatenate(
                [jnp.concatenate(ms, axis=1), jnp.concatenate(bws, axis=1)], axis=0)
            xp = act_ref[0, :, h0 * P:(h0 + 2) * P]
            rhs = jnp.concatenate(
                [jnp.where(lo_half, xp, 0.0), jnp.where(lo_half, 0.0, xp)], axis=0).astype(BF16)
            res = _dot(lhs, rhs)
            e_pair = jnp.where(lo_half, es[0], es[1])
            lsl = slice(2 * k * P, (2 * k + 2) * P)
            y_pair = res[:L] + e_pair * yoff[:, lsl]
            state_ref[g, :, lsl] = state_ref[g, :, lsl] * e_pair[r_tot:r_tot + 1, :] + res[L:]
            osl = slice(h0 * P, (h0 + 2) * P)
            if final:
                y_ref[:, osl] = y_pair + yb_ref[0, :, osl] + xp * dsk_ref[0:1, osl]
            else:
                out_ref[0, :, osl] = y_pair

    if final:
        gw = d_inner // G
        for g in range(G):
            gs = slice(g * gw, (g + 1) * gw)
            y = y_ref[:, gs] * _silu(z_ref[0, :, gs])
            ms_ = jnp.mean(y * y, axis=-1, keepdims=True)
            out_ref[0, :, gs] = (y * lax.rsqrt(ms_ + LN_EPS) * ng_ref[0:1, gs]).astype(out_ref.dtype)


def _ssd(act, cs, ctd, wT, finish, *, reverse, n_heads):
    b, t, conv_ch = act.shape
    L = SSD_CHUNK
    nc = t // L
    d_inner = n_heads * SSM_HEADDIM

    def chunk(c):
        return (nc - 1 - c) if reverse else c

    def blk(w):
        return pl.BlockSpec((1, L, w), lambda i, c: (i, chunk(c), 0))

    def const(w):
        return _resident((1, w), lambda i, c: (0, 0))

    tblk = pl.BlockSpec((1, 1, LANES, L), lambda i, c: (i, chunk(c), 0, 0))
    in_specs = [blk(conv_ch), blk(LANES), tblk, tblk]
    args = [act, cs, ctd, wT]
    scratch = [pltpu.VMEM((SSM_GROUPS, SSM_STATE, d_inner // SSM_GROUPS), F32)]
    if finish is not None:
        z, y_other, d_skip_l, norm_g = finish
        in_specs += [blk(d_inner), blk(d_inner), const(d_inner), const(d_inner)]
        args += [z, y_other, d_skip_l, norm_g.reshape(1, d_inner)]
        scratch.append(pltpu.VMEM((L, d_inner), F32))
    return pl.pallas_call(
        functools.partial(_ssd_kernel, reverse=reverse, final=finish is not None, n_heads=n_heads),
        grid=(b, nc),
        in_specs=in_specs,
        out_specs=blk(d_inner),
        out_shape=jax.ShapeDtypeStruct((b, t, d_inner), F32 if finish is None else BF16),
        scratch_shapes=scratch,
        compiler_params=_cparams("arbitrary", "arbitrary"),
        name="ssd_bwd" if reverse else "ssd_fwd_final",
    )(*args)


def _lane_pack(v2, n_heads):
    flat = v2.reshape(1, 2 * n_heads).astype(F32)
    return jnp.zeros((1, LANES), F32).at[:, :2 * n_heads].set(flat)


def _mamba2_bidir(x, mod, w_in, conv_w, conv_b, dt_bias, a_log, d_skip, norm_g, *, batch, seq):
    d = x.shape[1]
    n_heads = d_skip.shape[0]
    d_inner = n_heads * SSM_HEADDIM
    conv_ch = conv_w.shape[1]
    assert 2 * n_heads <= LANES
    n_main = d_inner + conv_ch
    w_main = w_in[:, :n_main].astype(BF16)
    w_dt = jnp.zeros((d, LANES), BF16).at[:, :2 * n_heads].set(w_in[:, n_main:].astype(BF16))
    dtb = _lane_pack(dt_bias, n_heads)
    alog = _lane_pack(a_log, n_heads)
    z, act, cs, ctd, wT = _ssm_in(x, mod, w_main, w_dt, conv_w, conv_b, dtb, alog, seq=seq, d_inner=d_inner,
                                  conv_ch=conv_ch, n_heads=n_heads)
    nc = seq // SSD_CHUNK
    z = z.reshape(batch, seq, d_inner)
    act = act.reshape(batch, seq, conv_ch)
    cs = cs.reshape(batch, seq, LANES)
    ctd = ctd.reshape(batch, nc, LANES, SSD_CHUNK)
    wT = wT.reshape(batch, nc, LANES, SSD_CHUNK)
    dsk = jnp.repeat(d_skip.astype(F32), SSM_HEADDIM).reshape(1, d_inner)
    yb = _ssd(act, cs, ctd, wT, None, reverse=True, n_heads=n_heads)
    y = _ssd(act, cs, ctd, wT, (z, yb, dsk, norm_g), reverse=False, n_heads=n_heads)
    return y.reshape(batch * seq, d_inner)


ALIBI_BLOCK = 256
N_EXT = 8
DEN_ROWS = 16
LOG2E = math.log2(math.e)


def _alibi_consts(n_heads):
    out = []
    for h in range(n_heads):
        c = 2.0 ** (-8.0 * (h + 1) / n_heads) * LOG2E
        hi = float(np.float32(c).astype(jnp.bfloat16))
        lo = float(np.float32(c - hi).astype(jnp.bfloat16))
        out.append((hi, lo))
    return out


def _qkv_kernel(x_ref, mod_ref, w_ref, qT_ref, k_ref, vT_ref, *, n_heads, dh, scale, per_b, consts):
    tm = x_ref.shape[0]
    u = _modulate(x_ref[...], mod_ref).astype(BF16)
    p = _dot(u, w_ref[...])
    hq = n_heads * LANES
    t0 = (pl.program_id(0) % per_b) * tm
    rq = lax.broadcasted_iota(jnp.int32, (LANES, tm), 0)
    pos_q = lax.broadcasted_iota(jnp.int32, (LANES, tm), 1) + t0
    q_lo = (pos_q & (ALIBI_BLOCK - 1)).astype(F32)
    q_hi = pos_q.astype(F32) - q_lo
    lk = lax.broadcasted_iota(jnp.int32, (tm, LANES), 1)
    pos_k = lax.broadcasted_iota(jnp.int32, (tm, LANES), 0) + t0
    k_lo = (pos_k & (ALIBI_BLOCK - 1)).astype(F32)
    k_hi = pos_k.astype(F32) - k_lo

    def ext_q(e0, hi, lo):
        r = rq - e0
        v = jnp.where(r < 2, hi, jnp.where(r < 4, lo, jnp.where((r & 1) == 0, -q_lo, -q_hi)))
        return jnp.where((r >= 0) & (r < N_EXT), v, 0.0)

    def ext_k(e0, hi, lo):
        l = lk - e0
        v = jnp.where(l < 4, jnp.where((l & 1) == 0, k_lo, k_hi), jnp.where(l < 6, hi, lo))
        return jnp.where((l >= 0) & (l < N_EXT), v, 0.0)

    for h in range(n_heads):
        hi, lo = consts[h]
        hs = slice(h * LANES, (h + 1) * LANES)
        qT = (p[:, hs] * (scale * LOG2E)).T
        kh = p[:, hq + h * LANES:hq + (h + 1) * LANES]
        qT_ref[0, h, 0] = jnp.where(rq < dh, qT, ext_q(dh, hi, lo)).astype(BF16)
        qT_ref[0, h, 1] = jnp.where(rq >= dh, qT, ext_q(0, hi, lo)).astype(BF16)
        k_ref[0, h, 0] = jnp.where(lk < dh, kh, ext_k(dh, hi, lo)).astype(BF16)
        k_ref[0, h, 1] = jnp.where(lk >= dh, kh, ext_k(0, hi, lo)).astype(BF16)
        vT_ref[0, h, 0:LANES] = p[:, 2 * hq + h * LANES:2 * hq + (h + 1) * LANES].T.astype(BF16)
        vT_ref[0, h, LANES:LANES + DEN_ROWS] = jnp.ones((DEN_ROWS, tm), BF16)


def _qkv(x, mod, w, *, batch, seq, n_heads, dh, scale, consts, tm=512):
    n, d = x.shape
    wn = w.shape[1]
    tm = _row_tile(seq, tm)
    per_b = seq // tm
    return pl.pallas_call(
        functools.partial(_qkv_kernel, n_heads=n_heads, dh=dh, scale=scale, per_b=per_b, consts=consts),
        grid=(n // tm,),
        in_specs=[
            pl.BlockSpec((tm, d), lambda i: (i, 0)),
            pl.BlockSpec((1, 3, d), lambda i: (i // per_b, 0, 0)),
            _resident((d, wn), lambda i: (0, 0)),
        ],
        out_specs=[
            pl.BlockSpec((1, n_heads, 2, LANES, tm), lambda i: (i // per_b, 0, 0, 0, i % per_b)),
            pl.BlockSpec((1, n_heads, 2, tm, LANES), lambda i: (i // per_b, 0, 0, i % per_b, 0)),
            pl.BlockSpec((1, n_heads, LANES + DEN_ROWS, tm), lambda i: (i // per_b, 0, 0, i % per_b)),
        ],
        out_shape=[
            jax.ShapeDtypeStruct((batch, n_heads, 2, LANES, seq), BF16),
            jax.ShapeDtypeStruct((batch, n_heads, 2, seq, LANES), BF16),
            jax.ShapeDtypeStruct((batch, n_heads, LANES + DEN_ROWS, seq), BF16),
        ],
        compiler_params=_cparams("arbitrary"),
        name="qkv",
    )(x, mod, w)


def _attn_kernel(c2_ref, lam_ref, g_ref, qp_ref, qn_ref, kc_ref, kn_ref, v_ref, o_ref,
                 sx_ref, sy_ref, mxx_ref, *, seq, dh, n_heads, n_blocks, lambda_init):
    g = pl.program_id(0)
    tb = ALIBI_BLOCK
    nk = seq // tb
    steps_per_bh = nk // 2
    h_cur = (g // steps_per_bh) % n_heads
    blk_even = (g % steps_per_bh) * 2
    nxt = jnp.minimum(2 * g + 2, n_blocks - 1)
    h_next = (nxt // nk) % n_heads
    blk_next = nxt % nk

    lam = lam_ref[...]
    lam_full = (jnp.exp(jnp.sum(lam[0:1] * lam[1:2], axis=-1, keepdims=True))
                - jnp.exp(jnp.sum(lam[2:3] * lam[3:4], axis=-1, keepdims=True)) + lambda_init)
    row = lax.broadcasted_iota(jnp.int32, (LANES, tb), 0)
    jj = lax.broadcasted_iota(jnp.int32, (tb, tb), 0)
    ii = lax.broadcasted_iota(jnp.int32, (tb, tb), 1)
    dist = (ii - jj).astype(F32)

    def scores_phase(q_maps, blk, head, k_ref, s_ref):
        q_vars = []
        for m in range(2):
            e0 = dh if m == 0 else 0
            q_left = q_maps[m]
            q_right = jnp.where((row >= e0) & (row < e0 + N_EXT), -q_left, q_left)
            q_vars.append((q_left, q_right))
        mx = [jnp.full((1, tb), NEG_BIG, F32), jnp.full((1, tb), NEG_BIG, F32)]

        def block(j):
            for m in range(2):
                q_var = jnp.where(j > blk, q_vars[m][1], q_vars[m][0])
                sT = _dot(k_ref[0, m, j * tb:(j + 1) * tb, :], q_var)
                s_ref[m, j * tb:(j + 1) * tb, :] = sT
                bm = jnp.max(sT, axis=0, keepdims=True)
                mx[m] = jnp.maximum(mx[m], jnp.where(j == blk, NEG_BIG, bm))

        def finish():
            fix = jnp.minimum(dist * c2_ref[head], 0.0)
            rows = pl.ds(pl.multiple_of(blk * tb, tb), tb)
            for m in range(2):
                sd = s_ref[m, rows, :] + fix
                s_ref[m, rows, :] = sd
                mx[m] = jnp.maximum(mx[m], jnp.max(sd, axis=0, keepdims=True))
            return mx

        return block, finish

    def values_phase(s_ref, mx, out_rows):
        acc = [jnp.zeros((LANES + DEN_ROWS, tb), F32), jnp.zeros((LANES + DEN_ROWS, tb), F32)]

        def block(j):
            for m in range(2):
                p = jnp.exp2(s_ref[m, j * tb:(j + 1) * tb, :] - mx[m])
                acc[m] = acc[m] + _dot(v_ref[0, :, j * tb:(j + 1) * tb], p.astype(BF16))

        def finish():
            outs = [a[:LANES] * (1.0 / a[LANES:LANES + 1]) for a in acc]
            oT = outs[0] - lam_full * outs[1]
            ms = jnp.mean(oT * oT, axis=0, keepdims=True)
            on = oT * lax.rsqrt(ms + LN_EPS) * g_ref[...]
            o_ref[out_rows, :] = on.T.astype(o_ref.dtype)

        return block, finish

    def run(phase_a, phase_b):
        for j in range(nk):
            phase_a[0](j)
            phase_b[0](j)
        res = phase_a[1]()
        phase_b[1]()
        return res

    @pl.when(g == 0)
    def _():
        blk, fin = scores_phase([qp_ref[0, 0, :, 0:tb], qp_ref[0, 1, :, 0:tb]], blk_even, h_cur, kc_ref, sx_ref)
        for j in range(nk):
            blk(j)
        mx0 = fin()
        mxx_ref[0] = mx0[0]
        mxx_ref[1] = mx0[1]

    mx_even = [mxx_ref[0], mxx_ref[1]]
    mx_odd = run(
        scores_phase([qp_ref[0, 0, :, tb:2 * tb], qp_ref[0, 1, :, tb:2 * tb]], blk_even + 1, h_cur, kc_ref, sy_ref),
        values_phase(sx_ref, mx_even, slice(0, tb)))
    mx_next = run(
        scores_phase([qn_ref[0, 0], qn_ref[0, 1]], blk_next, h_next, kn_ref, sx_ref),
        values_phase(sy_ref, mx_odd, slice(tb, 2 * tb)))
    mxx_ref[0] = mx_next[0]
    mxx_ref[1] = mx_next[1]


def _diff_attention(x, mod, w_qkv, lam, subln_g, *, batch, seq, lambda_init):
    n_heads = DA_HEADS
    dh = lam.shape[1]
    assert 2 * dh == LANES and w_qkv.shape[1] == 3 * n_heads * LANES and dh >= N_EXT
    tb = ALIBI_BLOCK
    nk = seq // tb
    assert seq % (2 * tb) == 0 and nk <= 256
    consts = _alibi_consts(n_heads)
    qT, k, vT = _qkv(x, mod, w_qkv.astype(BF16), batch=batch, seq=seq, n_heads=n_heads, dh=dh,
                     scale=dh ** -0.5, consts=consts)
    n_bh = batch * n_heads
    qT = qT.reshape(n_bh, 2, LANES, seq)
    k = k.reshape(n_bh, 2, seq, LANES)
    vT = vT.reshape(n_bh, LANES + DEN_ROWS, seq)
    spb = nk // 2
    n_blocks = n_bh * nk

    def nxt(g):
        return jnp.minimum(2 * g + 2, n_blocks - 1)

    c2 = jnp.asarray([2.0 * (hi + lo) for hi, lo in consts], F32)
    g_col = (subln_g.astype(F32) * (1.0 - lambda_init)).reshape(LANES, 1)
    return pl.pallas_call(
        functools.partial(_attn_kernel, seq=seq, dh=dh, n_heads=n_heads, n_blocks=n_blocks, lambda_init=lambda_init),
        grid=(n_bh * spb,),
        in_specs=[
            pl.BlockSpec(memory_space=pltpu.SMEM),
            _resident((4, dh), lambda g: (0, 0)),
            _resident((LANES, 1), lambda g: (0, 0)),
            pl.BlockSpec((1, 2, LANES, 2 * tb), lambda g: (g // spb, 0, 0, g % spb)),
            pl.BlockSpec((1, 2, LANES, tb), lambda g: (nxt(g) // nk, 0, 0, nxt(g) % nk)),
            pl.BlockSpec((1, 2, seq, LANES), lambda g: (g // spb, 0, 0, 0)),
            pl.BlockSpec((1, 2, seq, LANES), lambda g: (nxt(g) // nk, 0, 0, 0)),
            pl.BlockSpec((1, LANES + DEN_ROWS, seq), lambda g: (g // spb, 0, 0)),
        ],
        out_specs=pl.BlockSpec(
            (2 * tb, LANES), lambda g: ((g // spb // n_heads) * spb + g % spb, (g // spb) % n_heads)),
        out_shape=jax.ShapeDtypeStruct((batch * seq, n_heads * LANES), BF16),
        scratch_shapes=[
            pltpu.VMEM((2, seq, tb), F32),
            pltpu.VMEM((2, seq, tb), F32),
            pltpu.VMEM((2, 1, tb), F32),
        ],
        compiler_params=_cparams("arbitrary"),
        name="diff_attn",
    )(c2, lam.astype(F32), g_col, qT, qT, k, k, vT)


def kernel(x, c, ada_w, ada_b, ln_g, ln_b, ffn_w_gate, ffn_w_up, ffn_w_down, ssm_w_in, ssm_conv_w, ssm_conv_b, ssm_dt_bias, ssm_a_log, ssm_d, ssm_norm_g, ssm_w_out, attn_w_qkv, attn_lambda, attn_subln_g, attn_w_out):
    batch, seq, d = x.shape
    depth = ada_w.shape[0]
    alpha = (2 * depth) ** 0.25
    mods = _adaln(c, ada_w, ada_b)
    xf = x.reshape(batch * seq, d)
    wg, wu, wd = ffn_w_gate.astype(BF16), ffn_w_up.astype(BF16), ffn_w_down.astype(BF16)
    for i in range(depth):
        ffn = functools.partial(_ffn, wg=wg, wu=wu, wd=wd, layer=i, seq=seq, alpha=alpha)
        xf = ffn(xf, mods[i, :, 0], lng=ln_g[i, 0], lnb=ln_b[i, 0], sub=0)
        li = i // N_MIXERS
        if i % N_MIXERS == 0:
            y = _mamba2_bidir(xf, mods[i, :, 1], ssm_w_in[li], ssm_conv_w[li], ssm_conv_b[li], ssm_dt_bias[li],
                              ssm_a_log[li], ssm_d[li], ssm_norm_g[li], batch=batch, seq=seq)
            w_out = ssm_w_out[li]
        else:
            lambda_init = 0.8 - 0.6 * math.exp(-0.3 * i)
            y = _diff_attention(xf, mods[i, :, 1], attn_w_qkv[li], attn_lambda[li], attn_subln_g[li],
                                batch=batch, seq=seq, lambda_init=lambda_init)
            w_out = attn_w_out[li]
        xf = _proj_ln(y, xf, mods[i, :, 1], w_out.astype(BF16), ln_g[i, 1], ln_b[i, 1], seq=seq, alpha=alpha)
        xf = ffn(xf, mods[i, :, 2], lng=ln_g[i, 2], lnb=ln_b[i, 2], sub=1)
    return xf.reshape(batch, seq, d)
```

```python
import functools
import math

import jax
import jax.numpy as jnp
import numpy as np
from jax import lax
from jax.experimental import pallas as pl
from jax.experimental.pallas import tpu as pltpu

F32 = jnp.float32
BF16 = jnp.bfloat16
LN_EPS = 1e-5
N_SUB = 3
N_MIXERS = 2
SSM_GROUPS = 4
SSM_STATE = 128
SSM_HEADDIM = 64
SSD_CHUNK = 128
DA_HEADS = 8
LANES = 128
HALO = 8
VMEM_LIMIT = 56 * 1024 * 1024
NEG_BIG = -1e30
LOG2E = math.log2(math.e)


def _cparams(*sem):
    return pltpu.CompilerParams(dimension_semantics=sem, vmem_limit_bytes=VMEM_LIMIT)


def _resident(shape, index_map):
    return pl.BlockSpec(shape, index_map, pipeline_mode=pl.Buffered(1))


def _dot(a, b):
    return jnp.dot(a, b, preferred_element_type=F32)


def _dot_nt(a, b):
    return lax.dot_general(a, b, (((1,), (1,)), ((), ())), preferred_element_type=F32)


def _split3(a):
    hi = a.astype(BF16)
    r = a - hi.astype(F32)
    mid = r.astype(BF16)
    lo = (r - mid.astype(F32)).astype(BF16)
    return hi, mid, lo


def _silu(v):
    return v * jax.nn.sigmoid(v)


def _layer_norm(z, g, b):
    mu = jnp.mean(z, axis=-1, keepdims=True)
    zc = z - mu
    var = jnp.mean(zc * zc, axis=-1, keepdims=True)
    return zc * lax.rsqrt(var + LN_EPS) * g + b


def _modulate(x, mod_ref):
    return x * (1.0 + mod_ref[0, 1:2, :]) + mod_ref[0, 0:1, :]


def _adaln_kernel(c_ref, w_ref, b_ref, o_ref):
    cond = _silu(c_ref[...])
    c_hi, c_lo, _ = _split3(cond)
    w_hi, w_lo, _ = _split3(w_ref[0])
    acc = _dot(c_hi, w_hi) + (_dot(c_hi, w_lo) + _dot(c_lo, w_hi))
    o_ref[0] = acc + b_ref[0]


def _adaln(c, ada_w, ada_b):
    depth, d, n = ada_w.shape
    b = c.shape[0]
    rows = -(-b // HALO) * HALO
    c_pad = jnp.zeros((rows, d), F32).at[:b].set(c)
    tn = 1024 if n % 1024 == 0 else n
    out = pl.pallas_call(
        _adaln_kernel,
        grid=(depth, n // tn),
        in_specs=[
            pl.BlockSpec((rows, d), lambda i, j: (0, 0)),
            pl.BlockSpec((1, d, tn), lambda i, j: (i, 0, j)),
            pl.BlockSpec((1, 1, tn), lambda i, j: (i, 0, j)),
        ],
        out_specs=pl.BlockSpec((1, rows, tn), lambda i, j: (i, 0, j)),
        out_shape=jax.ShapeDtypeStruct((depth, rows, n), F32),
        compiler_params=_cparams("arbitrary", "arbitrary"),
        name="adaln",
    )(c_pad, ada_w, ada_b.reshape(depth, 1, n))
    return out[:, :b].reshape(depth, b, N_SUB, 3, d)


def _ffn_kernel(x_ref, mod_ref, wg_ref, wu_ref, wd_ref, lng_ref, lnb_ref, o_ref, *, alpha, n_sub):
    rows = x_ref.shape[0] // n_sub
    for s in range(n_sub):
        rs = slice(s * rows, (s + 1) * rows)
        x = x_ref[rs, :]
        u = _modulate(x, mod_ref).astype(BF16)
        g = _dot(u, wg_ref[...])
        v = _dot(u, wu_ref[...])
        h = (_silu(g) * v).astype(BF16)
        y = _dot(h, wd_ref[...])
        z = alpha * x + (0.5 * (1.0 + mod_ref[0, 2:3, :])) * y
        o_ref[rs, :] = _layer_norm(z, lng_ref[...], lnb_ref[...])


def _row_tile(t, want):
    tm = min(want, t)
    assert t % tm == 0
    return tm


def _ffn(x, mod, wg, wu, wd, lng, lnb, *, layer, sub, seq, alpha, tm=1024, sub_rows=256):
    n, d = x.shape
    f = wg.shape[-1]
    tm = _row_tile(seq, tm)
    n_sub = max(tm // sub_rows, 1)
    per_b = seq // tm
    return pl.pallas_call(
        functools.partial(_ffn_kernel, alpha=alpha, n_sub=n_sub),
        grid=(n // tm,),
        in_specs=[
            pl.BlockSpec((tm, d), lambda i: (i, 0)),
            pl.BlockSpec((1, 3, d), lambda i: (i // per_b, 0, 0)),
            _resident((None, None, d, f), lambda i: (layer, sub, 0, 0)),
            _resident((None, None, d, f), lambda i: (layer, sub, 0, 0)),
            _resident((None, None, f, d), lambda i: (layer, sub, 0, 0)),
            _resident((1, d), lambda i: (0, 0)),
            _resident((1, d), lambda i: (0, 0)),
        ],
        out_specs=pl.BlockSpec((tm, d), lambda i: (i, 0)),
        out_shape=jax.ShapeDtypeStruct((n, d), F32),
        compiler_params=_cparams("arbitrary"),
        name="ffn_ln",
    )(x, mod, wg, wu, wd, lng.reshape(1, d), lnb.reshape(1, d))


def _proj_ln_kernel(a_ref, x_ref, mod_ref, w_ref, lng_ref, lnb_ref, o_ref, *, alpha, n_sub):
    rows = x_ref.shape[0] // n_sub
    for s in range(n_sub):
        rs = slice(s * rows, (s + 1) * rows)
        y = _dot(a_ref[rs, :], w_ref[...])
        z = alpha * x_ref[rs, :] + (1.0 + mod_ref[0, 2:3, :]) * y
        o_ref[rs, :] = _layer_norm(z, lng_ref[...], lnb_ref[...])


def _proj_ln(a, x, mod, w, lng, lnb, *, seq, alpha, tm=1024, sub_rows=256):
    n, d = x.shape
    k = a.shape[1]
    tm = _row_tile(seq, tm)
    n_sub = max(tm // sub_rows, 1)
    per_b = seq // tm
    return pl.pallas_call(
        functools.partial(_proj_ln_kernel, alpha=alpha, n_sub=n_sub),
        grid=(n // tm,),
        in_specs=[
            pl.BlockSpec((tm, k), lambda i: (i, 0)),
            pl.BlockSpec((tm, d), lambda i: (i, 0)),
            pl.BlockSpec((1, 3, d), lambda i: (i // per_b, 0, 0)),
            _resident((k, d), lambda i: (0, 0)),
            _resident((1, d), lambda i: (0, 0)),
            _resident((1, d), lambda i: (0, 0)),
        ],
        out_specs=pl.BlockSpec((tm, d), lambda i: (i, 0)),
        out_shape=jax.ShapeDtypeStruct((n, d), F32),
        compiler_params=_cparams("arbitrary"),
        name="proj_ln",
    )(a, x, mod, w, lng.reshape(1, d), lnb.reshape(1, d))


def _ssm_in_kernel(x_ref, xp_ref, xn_ref, mod_ref, w_ref, cw_ref, cb_ref, dtb_ref, alog_ref,
                   z_ref, act_ref, cs_ref, ctd_ref, wT_ref, p_ref, *, d_inner, conv_ch, per_b, n_heads):
    tm = x_ref.shape[0]
    i = pl.program_id(0)
    first = (i % per_b) == 0
    last = (i % per_b) == per_b - 1
    u_main = _modulate(x_ref[...], mod_ref)
    u_prev = jnp.where(first, 0.0, _modulate(xp_ref[...], mod_ref))
    u_next = jnp.where(last, 0.0, _modulate(xn_ref[...], mod_ref))
    u = u_main.astype(BF16)
    u_halo = jnp.concatenate([u_prev, u_main, u_next], axis=0).astype(BF16)
    rows = tm + 2 * HALO
    cblk = p_ref.shape[2]
    n_conv = conv_ch // cblk
    n_z = d_inner // cblk

    def project(k):
        c0 = d_inner + k * cblk
        p_ref[k % 2] = _dot(u_halo, w_ref[:, c0:c0 + cblk])

    def conv_act(k):
        cs_ = slice(k * cblk, (k + 1) * cblk)
        p = p_ref[k % 2]
        p_prev = pltpu.roll(p, 1, axis=0)[HALO:HALO + tm]
        p_next = pltpu.roll(p, rows - 1, axis=0)[HALO:HALO + tm]
        conv = (p_prev * cw_ref[0:1, cs_] + p[HALO:HALO + tm] * cw_ref[1:2, cs_]
                + p_next * cw_ref[2:3, cs_] + cb_ref[0:1, cs_])
        act_ref[:, cs_] = _silu(conv)

    def z_chunk(j):
        z_ref[:, j * cblk:(j + 1) * cblk] = _dot(u, w_ref[:, j * cblk:(j + 1) * cblk])

    project(0)

    L = SSD_CHUNK
    x_dt = _dot(u, w_ref[:, d_inner + conv_ch:]) + dtb_ref[...]
    dt = jnp.maximum(x_dt, 0.0) + jnp.log1p(jnp.exp(-jnp.abs(x_dt)))
    a = dt * (-jnp.exp(alog_ref[...]))
    ri = lax.broadcasted_iota(jnp.int32, (L, L), 0)
    ci = lax.broadcasted_iota(jnp.int32, (L, L), 1)
    tri = jnp.where(ci <= ri, 1.0, 0.0).astype(BF16)
    fwd_lane = lax.broadcasted_iota(jnp.int32, (L, LANES), 1) < n_heads
    fwd_row = lax.broadcasted_iota(jnp.int32, (LANES, L), 0) < n_heads
    for k in range(tm // L):
        rs = slice(k * L, (k + 1) * L)
        a_k = a[rs]
        a_hi, a_mid, a_lo = _split3(a_k)
        cs_f = _dot(tri, a_hi) + (_dot(tri, a_mid) + _dot(tri, a_lo))
        cs_b = cs_f[L - 1:L, :] - cs_f + a_k
        cs = jnp.where(fwd_lane, cs_f, cs_b)
        csT = cs.T
        dtT = dt[rs].T
        totT = jnp.where(fwd_row, csT[:, L - 1:L], csT[:, 0:1])
        cs_ref[rs, :] = cs * LOG2E
        ctd_ref[k] = (csT - jnp.log(dtT)) * LOG2E
        wT_ref[k] = jnp.exp(totT - csT) * dtT

    for k in range(n_conv):
        if k + 1 < n_conv:
            project(k + 1)
        conv_act(k)
        for j in range(k * n_z // n_conv, (k + 1) * n_z // n_conv):
            z_chunk(j)


def _ssm_in(x, mod, w_pad, conv_w, conv_b, dt_bias_l, a_log_l, *, seq, d_inner, conv_ch, n_heads, tm=512, cblk=256):
    n, d = x.shape
    tm = _row_tile(seq, tm)
    per_b = seq // tm
    hb = tm // HALO
    n_hb = n // HALO
    L = SSD_CHUNK
    assert conv_ch % cblk == 0 and d_inner % cblk == 0 and tm % L == 0
    return pl.pallas_call(
        functools.partial(_ssm_in_kernel, d_inner=d_inner, conv_ch=conv_ch, per_b=per_b, n_heads=n_heads),
        grid=(n // tm,),
        in_specs=[
            pl.BlockSpec((tm, d), lambda i: (i, 0)),
            pl.BlockSpec((HALO, d), lambda i: (jnp.maximum(i * hb - 1, 0), 0)),
            pl.BlockSpec((HALO, d), lambda i: (jnp.minimum((i + 1) * hb, n_hb - 1), 0)),
            pl.BlockSpec((1, 3, d), lambda i: (i // per_b, 0, 0)),
            _resident((d, d_inner + conv_ch + LANES), lambda i: (0, 0)),
            _resident((3, conv_ch), lambda i: (0, 0)),
            _resident((1, conv_ch), lambda i: (0, 0)),
            _resident((1, LANES), lambda i: (0, 0)),
            _resident((1, LANES), lambda i: (0, 0)),
        ],
        out_specs=[
            pl.BlockSpec((tm, d_inner), lambda i: (i, 0)),
            pl.BlockSpec((tm, conv_ch), lambda i: (i, 0)),
            pl.BlockSpec((tm, LANES), lambda i: (i, 0)),
            pl.BlockSpec((tm // L, LANES, L), lambda i: (i, 0, 0)),
            pl.BlockSpec((tm // L, LANES, L), lambda i: (i, 0, 0)),
        ],
        out_shape=[
            jax.ShapeDtypeStruct((n, d_inner), F32),
            jax.ShapeDtypeStruct((n, conv_ch), F32),
            jax.ShapeDtypeStruct((n, LANES), F32),
            jax.ShapeDtypeStruct((n // L, LANES, L), F32),
            jax.ShapeDtypeStruct((n // L, LANES, L), F32),
        ],
        scratch_shapes=[pltpu.VMEM((2, tm + 2 * HALO, cblk), F32)],
        compiler_params=_cparams("arbitrary"),
        name="ssm_in",
    )(x, x, x, mod, w_pad, conv_w, conv_b.reshape(1, conv_ch), dt_bias_l, a_log_l)


def _ssd_kernel(*refs, reverse, final, n_heads):
    if final:
        act_ref, cs_ref, ctd_ref, wT_ref, z_ref, yb_ref, dsk_ref, ng_ref, out_ref, state_ref, y_ref = refs
    else:
        act_ref, cs_ref, ctd_ref, wT_ref, out_ref, state_ref = refs
    L = SSD_CHUNK
    P = SSM_HEADDIM
    G = SSM_GROUPS
    NS = SSM_STATE
    H = n_heads
    HPG = H // G
    d_inner = H * P
    assert L == 2 * P == LANES and HPG % 2 == 0

    @pl.when(pl.program_id(1) == 0)
    def _():
        state_ref[...] = jnp.zeros_like(state_ref)

    lane0 = H if reverse else 0
    cs = cs_ref[0]
    csT_dt = ctd_ref[0, 0]
    wT = wT_ref[0, 0]
    ri = lax.broadcasted_iota(jnp.int32, (L, L), 0)
    ci = lax.broadcasted_iota(jnp.int32, (L, L), 1)
    mask = (ci >= ri) if reverse else (ci <= ri)
    r_tot = 0 if reverse else L - 1
    lane = lax.broadcasted_iota(jnp.int32, (L, LANES), 1)
    lo_half = lane < P

    for g in range(G):
        b0 = d_inner + g * NS
        c0 = d_inner + G * NS + g * NS
        Bg = act_ref[0, :, b0:b0 + NS]
        Cg = act_ref[0, :, c0:c0 + NS].astype(BF16)
        CB = _dot_nt(Cg, Bg.astype(BF16))
        BgT = Bg.T
        yoff = _dot(Cg, state_ref[g].astype(BF16))
        for k in range(HPG // 2):
            h0 = g * HPG + 2 * k
            ms, bws, es = [], [], []
            for h in (h0, h0 + 1):
                hl = lane0 + h
                colb = jnp.broadcast_to(cs[:, hl:hl + 1], (L, L))
                seg = colb - csT_dt[hl:hl + 1, :]
                lm = jnp.exp2(jnp.where(mask, seg, NEG_BIG))
                ms.append((CB * lm).astype(BF16))
                bws.append((BgT * wT[hl:hl + 1, :]).astype(BF16))
                es.append(jnp.exp2(colb))
            lhs = jnp.concatenate(
                [jnp.concatenate(ms, axis=1), jnp.concatenate(bws, axis=1)], axis=0)
            xp = act_ref[0, :, h0 * P:(h0 + 2) * P]
            rhs = jnp.concatenate(
                [jnp.where(lo_half, xp, 0.0), jnp.where(lo_half, 0.0, xp)], axis=0).astype(BF16)
            res = _dot(lhs, rhs)
            e_pair = jnp.where(lo_half, es[0], es[1])
            lsl = slice(2 * k * P, (2 * k + 2) * P)
            y_pair = res[:L] + e_pair * yoff[:, lsl]
            state_ref[g, :, lsl] = state_ref[g, :, lsl] * e_pair[r_tot:r_tot + 1, :] + res[L:]
            osl = slice(h0 * P, (h0 + 2) * P)
            if final:
                y_ref[:, osl] = y_pair + yb_ref[0, :, osl] + xp * dsk_ref[0:1, osl]
            else:
                out_ref[0, :, osl] = y_pair

    if final:
        gw = d_inner // G
        for g in range(G):
            gs = slice(g * gw, (g + 1) * gw)
            y = y_ref[:, gs] * _silu(z_ref[0, :, gs])
            ms_ = jnp.mean(y * y, axis=-1, keepdims=True)
            out_ref[0, :, gs] = (y * lax.rsqrt(ms_ + LN_EPS) * ng_ref[0:1, gs]).astype(out_ref.dtype)


def _ssd(act, cs, ctd, wT, finish, *, reverse, n_heads):
    b, t, conv_ch = act.shape
    L = SSD_CHUNK
    nc = t // L
    d_inner = n_heads * SSM_HEADDIM

    def chunk(c):
        return (nc - 1 - c) if reverse else c

    def blk(w):
        return pl.BlockSpec((1, L, w), lambda i, c: (i, chunk(c), 0))

    def const(w):
        return _resident((1, w), lambda i, c: (0, 0))

    tblk = pl.BlockSpec((1, 1, LANES, L), lambda i, c: (i, chunk(c), 0, 0))
    in_specs = [blk(conv_ch), blk(LANES), tblk, tblk]
    args = [act, cs, ctd, wT]
    scratch = [pltpu.VMEM((SSM_GROUPS, SSM_STATE, d_inner // SSM_GROUPS), F32)]
    if finish is not None:
        z, y_other, d_skip_l, norm_g = finish
        in_specs += [blk(d_inner), blk(d_inner), const(d_inner), const(d_inner)]
        args += [z, y_other, d_skip_l, norm_g.reshape(1, d_inner)]
        scratch.append(pltpu.VMEM((L, d_inner), F32))
    return pl.pallas_call(
        functools.partial(_ssd_kernel, reverse=reverse, final=finish is not None, n_heads=n_heads),
        grid=(b, nc),
        in_specs=in_specs,
        out_specs=blk(d_inner),
        out_shape=jax.ShapeDtypeStruct((b, t, d_inner), F32 if finish is None else BF16),
        scratch_shapes=scratch,
        compiler_params=_cparams("arbitrary", "arbitrary"),
        name="ssd_bwd" if reverse else "ssd_fwd_final",
    )(*args)


def _lane_pack(v2, n_heads):
    flat = v2.reshape(1, 2 * n_heads).astype(F32)
    return jnp.zeros((1, LANES), F32).at[:, :2 * n_heads].set(flat)


def _mamba2_bidir(x, mod, w_in, conv_w, conv_b, dt_bias, a_log, d_skip, norm_g, *, batch, seq):
    d = x.shape[1]
    n_heads = d_skip.shape[0]
    d_inner = n_heads * SSM_HEADDIM
    conv_ch = conv_w.shape[1]
    assert 2 * n_heads <= LANES
    w_pad = jnp.zeros((d, d_inner + conv_ch + LANES), BF16).at[:, :w_in.shape[1]].set(w_in.astype(BF16))
    dtb = _lane_pack(dt_bias, n_heads)
    alog = _lane_pack(a_log, n_heads)
    z, act, cs, ctd, wT = _ssm_in(x, mod, w_pad, conv_w, conv_b, dtb, alog, seq=seq, d_inner=d_inner,
                                  conv_ch=conv_ch, n_heads=n_heads)
    nc = seq // SSD_CHUNK
    z = z.reshape(batch, seq, d_inner)
    act = act.reshape(batch, seq, conv_ch)
    cs = cs.reshape(batch, seq, LANES)
    ctd = ctd.reshape(batch, nc, LANES, SSD_CHUNK)
    wT = wT.reshape(batch, nc, LANES, SSD_CHUNK)
    dsk = jnp.repeat(d_skip.astype(F32), SSM_HEADDIM).reshape(1, d_inner)
    yb = _ssd(act, cs, ctd, wT, None, reverse=True, n_heads=n_heads)
    y = _ssd(act, cs, ctd, wT, (z, yb, dsk, norm_g), reverse=False, n_heads=n_heads)
    return y.reshape(batch * seq, d_inner)


ALIBI_BLOCK = 256
N_EXT = 8
DEN_ROWS = 16


def _alibi_consts(n_heads):
    out = []
    for h in range(n_heads):
        c = 2.0 ** (-8.0 * (h + 1) / n_heads) * LOG2E
        hi = float(np.float32(c).astype(jnp.bfloat16))
        lo = float(np.float32(c - hi).astype(jnp.bfloat16))
        out.append((hi, lo))
    return out


def _qkv_kernel(x_ref, mod_ref, w_ref, qT_ref, k_ref, vT_ref, *, n_heads, dh, scale, per_b, consts):
    tm = x_ref.shape[0]
    u = _modulate(x_ref[...], mod_ref).astype(BF16)
    p = _dot(u, w_ref[...])
    hq = n_heads * LANES
    t0 = (pl.program_id(0) % per_b) * tm
    rq = lax.broadcasted_iota(jnp.int32, (LANES, tm), 0)
    pos_q = lax.broadcasted_iota(jnp.int32, (LANES, tm), 1) + t0
    q_lo = (pos_q & (ALIBI_BLOCK - 1)).astype(F32)
    q_hi = pos_q.astype(F32) - q_lo
    lk = lax.broadcasted_iota(jnp.int32, (tm, LANES), 1)
    pos_k = lax.broadcasted_iota(jnp.int32, (tm, LANES), 0) + t0
    k_lo = (pos_k & (ALIBI_BLOCK - 1)).astype(F32)
    k_hi = pos_k.astype(F32) - k_lo

    def ext_q(e0, hi, lo):
        r = rq - e0
        v = jnp.where(r < 2, hi, jnp.where(r < 4, lo, jnp.where((r & 1) == 0, -q_lo, -q_hi)))
        return jnp.where((r >= 0) & (r < N_EXT), v, 0.0)

    def ext_k(e0, hi, lo):
        l = lk - e0
        v = jnp.where(l < 4, jnp.where((l & 1) == 0, k_lo, k_hi), jnp.where(l < 6, hi, lo))
        return jnp.where((l >= 0) & (l < N_EXT), v, 0.0)

    for h in range(n_heads):
        hi, lo = consts[h]
        hs = slice(h * LANES, (h + 1) * LANES)
        qT = (p[:, hs] * (scale * LOG2E)).T
        kh = p[:, hq + h * LANES:hq + (h + 1) * LANES]
        qT_ref[0, h, 0] = jnp.where(rq < dh, qT, ext_q(dh, hi, lo)).astype(BF16)
        qT_ref[0, h, 1] = jnp.where(rq >= dh, qT, ext_q(0, hi, lo)).astype(BF16)
        k_ref[0, h, 0] = jnp.where(lk < dh, kh, ext_k(dh, hi, lo)).astype(BF16)
        k_ref[0, h, 1] = jnp.where(lk >= dh, kh, ext_k(0, hi, lo)).astype(BF16)
        vT_ref[0, h, 0:LANES] = p[:, 2 * hq + h * LANES:2 * hq + (h + 1) * LANES].T.astype(BF16)
        vT_ref[0, h, LANES:LANES + DEN_ROWS] = jnp.ones((DEN_ROWS, tm), BF16)


def _qkv(x, mod, w, *, batch, seq, n_heads, dh, scale, consts, tm=512):
    n, d = x.shape
    wn = w.shape[1]
    tm = _row_tile(seq, tm)
    per_b = seq // tm
    return pl.pallas_call(
        functools.partial(_qkv_kernel, n_heads=n_heads, dh=dh, scale=scale, per_b=per_b, consts=consts),
        grid=(n // tm,),
        in_specs=[
            pl.BlockSpec((tm, d), lambda i: (i, 0)),
            pl.BlockSpec((1, 3, d), lambda i: (i // per_b, 0, 0)),
            _resident((d, wn), lambda i: (0, 0)),
        ],
        out_specs=[
            pl.BlockSpec((1, n_heads, 2, LANES, tm), lambda i: (i // per_b, 0, 0, 0, i % per_b)),
            pl.BlockSpec((1, n_heads, 2, tm, LANES), lambda i: (i // per_b, 0, 0, i % per_b, 0)),
            pl.BlockSpec((1, n_heads, LANES + DEN_ROWS, tm), lambda i: (i // per_b, 0, 0, i % per_b)),
        ],
        out_shape=[
            jax.ShapeDtypeStruct((batch, n_heads, 2, LANES, seq), BF16),
            jax.ShapeDtypeStruct((batch, n_heads, 2, seq, LANES), BF16),
            jax.ShapeDtypeStruct((batch, n_heads, LANES + DEN_ROWS, seq), BF16),
        ],
        compiler_params=_cparams("arbitrary"),
        name="qkv",
    )(x, mod, w)


def _attn_kernel(c2_ref, lam_ref, g_ref, qp_ref, qn_ref, kc_ref, kn_ref, v_ref, o_ref,
                 sx_ref, sy_ref, mxx_ref, *, seq, dh, n_heads, n_blocks, nb, lambda_init):
    g = pl.program_id(0)
    tb = ALIBI_BLOCK
    nk = seq // tb
    steps_per_bh = nk // nb
    h_cur = (g // steps_per_bh) % n_heads
    blk_first = (g % steps_per_bh) * nb
    nxt = jnp.minimum(nb * (g + 1), n_blocks - 1)
    h_next = (nxt // nk) % n_heads
    blk_next = nxt % nk

    lam = lam_ref[...]
    lam_full = (jnp.exp(jnp.sum(lam[0:1] * lam[1:2], axis=-1, keepdims=True))
                - jnp.exp(jnp.sum(lam[2:3] * lam[3:4], axis=-1, keepdims=True)) + lambda_init)
    row = lax.broadcasted_iota(jnp.int32, (LANES, tb), 0)
    jj = lax.broadcasted_iota(jnp.int32, (tb, tb), 0)
    ii = lax.broadcasted_iota(jnp.int32, (tb, tb), 1)
    dist = (ii - jj).astype(F32)

    def scores_phase(q_maps, blk, head, k_ref, s_ref):
        q_vars = []
        for m in range(2):
            e0 = dh if m == 0 else 0
            q_left = q_maps[m]
            q_right = jnp.where((row >= e0) & (row < e0 + N_EXT), -q_left, q_left)
            q_vars.append((q_left, q_right))
        mx = [jnp.full((1, tb), NEG_BIG, F32), jnp.full((1, tb), NEG_BIG, F32)]

        def block(j):
            for m in range(2):
                q_var = jnp.where(j > blk, q_vars[m][1], q_vars[m][0])
                sT = _dot(k_ref[0, m, j * tb:(j + 1) * tb, :], q_var)
                s_ref[m, j * tb:(j + 1) * tb, :] = sT
                bm = jnp.max(sT, axis=0, keepdims=True)
                mx[m] = jnp.maximum(mx[m], jnp.where(j == blk, NEG_BIG, bm))

        def finish():
            fix = jnp.minimum(dist * c2_ref[head], 0.0)
            rows = pl.ds(pl.multiple_of(blk * tb, tb), tb)
            for m in range(2):
                sd = s_ref[m, rows, :] + fix
                s_ref[m, rows, :] = sd
                mx[m] = jnp.maximum(mx[m], jnp.max(sd, axis=0, keepdims=True))
            return mx

        return block, finish

    def values_phase(s_ref, mx, out_rows):
        acc = [jnp.zeros((LANES + DEN_ROWS, tb), F32), jnp.zeros((LANES + DEN_ROWS, tb), F32)]

        def block(j):
            for m in range(2):
                p = jnp.exp2(s_ref[m, j * tb:(j + 1) * tb, :] - mx[m])
                acc[m] = acc[m] + _dot(v_ref[0, :, j * tb:(j + 1) * tb], p.astype(BF16))

        def finish():
            outs = [a[:LANES] * (1.0 / a[LANES:LANES + 1]) for a in acc]
            oT = outs[0] - lam_full * outs[1]
            ms = jnp.mean(oT * oT, axis=0, keepdims=True)
            on = oT * lax.rsqrt(ms + LN_EPS) * g_ref[...]
            o_ref[out_rows, :] = on.T.astype(o_ref.dtype)

        return block, finish

    def run(phase_a, phase_b):
        for j in range(nk):
            phase_a[0](j)
            phase_b[0](j)
        res = phase_a[1]()
        phase_b[1]()
        return res

    def q_block(r):
        return [qp_ref[0, 0, :, r * tb:(r + 1) * tb], qp_ref[0, 1, :, r * tb:(r + 1) * tb]]

    @pl.when(g == 0)
    def _():
        blk, fin = scores_phase(q_block(0), blk_first, h_cur, kc_ref, sx_ref)
        for j in range(nk):
            blk(j)
        mx0 = fin()
        mxx_ref[0] = mx0[0]
        mxx_ref[1] = mx0[1]

    mx = [mxx_ref[0], mxx_ref[1]]
    s_bufs = (sx_ref, sy_ref)
    for r in range(nb):
        s_cur, s_nxt = s_bufs[r % 2], s_bufs[(r + 1) % 2]
        if r + 1 < nb:
            scores = scores_phase(q_block(r + 1), blk_first + r + 1, h_cur, kc_ref, s_nxt)
        else:
            scores = scores_phase([qn_ref[0, 0], qn_ref[0, 1]], blk_next, h_next, kn_ref, s_nxt)
        mx = run(scores, values_phase(s_cur, mx, slice(r * tb, (r + 1) * tb)))
    mxx_ref[0] = mx[0]
    mxx_ref[1] = mx[1]


def _diff_attention(x, mod, w_qkv, lam, subln_g, *, batch, seq, lambda_init, nb=4):
    n_heads = DA_HEADS
    dh = lam.shape[1]
    assert 2 * dh == LANES and w_qkv.shape[1] == 3 * n_heads * LANES and dh >= N_EXT
    tb = ALIBI_BLOCK
    nk = seq // tb
    nb = min(nb, nk)
    assert nb % 2 == 0 and seq % (nb * tb) == 0 and nk <= 256
    consts = _alibi_consts(n_heads)
    qT, k, vT = _qkv(x, mod, w_qkv.astype(BF16), batch=batch, seq=seq, n_heads=n_heads, dh=dh,
                     scale=dh ** -0.5, consts=consts)
    n_bh = batch * n_heads
    qT = qT.reshape(n_bh, 2, LANES, seq)
    k = k.reshape(n_bh, 2, seq, LANES)
    vT = vT.reshape(n_bh, LANES + DEN_ROWS, seq)
    spb = nk // nb
    n_blocks = n_bh * nk

    def nxt(g):
        return jnp.minimum(nb * (g + 1), n_blocks - 1)

    c2 = jnp.asarray([2.0 * (hi + lo) for hi, lo in consts], F32)
    g_col = (subln_g.astype(F32) * (1.0 - lambda_init)).reshape(LANES, 1)
    return pl.pallas_call(
        functools.partial(_attn_kernel, seq=seq, dh=dh, n_heads=n_heads, n_blocks=n_blocks, nb=nb,
                          lambda_init=lambda_init),
        grid=(n_bh * spb,),
        in_specs=[
            pl.BlockSpec(memory_space=pltpu.SMEM),
            _resident((4, dh), lambda g: (0, 0)),
            _resident((LANES, 1), lambda g: (0, 0)),
            pl.BlockSpec((1, 2, LANES, nb * tb), lambda g: (g // spb, 0, 0, g % spb)),
            pl.BlockSpec((1, 2, LANES, tb), lambda g: (nxt(g) // nk, 0, 0, nxt(g) % nk)),
            pl.BlockSpec((1, 2, seq, LANES), lambda g: (g // spb, 0, 0, 0)),
            pl.BlockSpec((1, 2, seq, LANES), lambda g: (nxt(g) // nk, 0, 0, 0)),
            pl.BlockSpec((1, LANES + DEN_ROWS, seq), lambda g: (g // spb, 0, 0)),
        ],
        out_specs=pl.BlockSpec(
            (nb * tb, LANES), lambda g: ((g // spb // n_heads) * spb + g % spb, (g // spb) % n_heads)),
        out_shape=jax.ShapeDtypeStruct((batch * seq, n_heads * LANES), BF16),
        scratch_shapes=[
            pltpu.VMEM((2, seq, tb), F32),
            pltpu.VMEM((2, seq, tb), F32),
            pltpu.VMEM((2, 1, tb), F32),
        ],
        compiler_params=_cparams("arbitrary"),
        name="diff_attn",
    )(c2, lam.astype(F32), g_col, qT, qT, k, k, vT)


def kernel(x, c, ada_w, ada_b, ln_g, ln_b, ffn_w_gate, ffn_w_up, ffn_w_down, ssm_w_in, ssm_conv_w, ssm_conv_b, ssm_dt_bias, ssm_a_log, ssm_d, ssm_norm_g, ssm_w_out, attn_w_qkv, attn_lambda, attn_subln_g, attn_w_out):
    batch, seq, d = x.shape
    depth = ada_w.shape[0]
    alpha = (2 * depth) ** 0.25
    mods = _adaln(c, ada_w, ada_b)
    xf = x.reshape(batch * seq, d)
    wg, wu, wd = ffn_w_gate.astype(BF16), ffn_w_up.astype(BF16), ffn_w_down.astype(BF16)
    for i in range(depth):
        ffn = functools.partial(_ffn, wg=wg, wu=wu, wd=wd, layer=i, seq=seq, alpha=alpha)
        xf = ffn(xf, mods[i, :, 0], lng=ln_g[i, 0], lnb=ln_b[i, 0], sub=0)
        li = i // N_MIXERS
        if i % N_MIXERS == 0:
            y = _mamba2_bidir(xf, mods[i, :, 1], ssm_w_in[li], ssm_conv_w[li], ssm_conv_b[li], ssm_dt_bias[li],
                              ssm_a_log[li], ssm_d[li], ssm_norm_g[li], batch=batch, seq=seq)
            w_out = ssm_w_out[li]
        else:
            lambda_init = 0.8 - 0.6 * math.exp(-0.3 * i)
            y = _diff_attention(xf, mods[i, :, 1], attn_w_qkv[li], attn_lambda[li], attn_subln_g[li],
                                batch=batch, seq=seq, lambda_init=lambda_init)
            w_out = attn_w_out[li]
        xf = _proj_ln(y, xf, mods[i, :, 1], w_out.astype(BF16), ln_g[i, 1], ln_b[i, 1], seq=seq, alpha=alpha)
        xf = ffn(xf, mods[i, :, 2], lng=ln_g[i, 2], lnb=ln_b[i, 2], sub=1)
    return xf.reshape(batch, seq, d)
```

```python
import functools
import math

import jax
import jax.numpy as jnp
import numpy as np
from jax import lax
from jax.experimental import pallas as pl
from jax.experimental.pallas import tpu as pltpu

F32 = jnp.float32
BF16 = jnp.bfloat16
LN_EPS = 1e-5
N_SUB = 3
N_MIXERS = 2
SSM_GROUPS = 4
SSM_STATE = 128
SSM_HEADDIM = 64
SSD_CHUNK = 128
DA_HEADS = 8
LANES = 128
HALO = 8
VMEM_LIMIT = 56 * 1024 * 1024
NEG_BIG = -1e30
LOG2E = math.log2(math.e)


def _cparams(*sem):
    return pltpu.CompilerParams(dimension_semantics=sem, vmem_limit_bytes=VMEM_LIMIT)


def _resident(shape, index_map):
    return pl.BlockSpec(shape, index_map, pipeline_mode=pl.Buffered(1))


def _dot(a, b):
    return jnp.dot(a, b, preferred_element_type=F32)


def _dot_nt(a, b):
    return lax.dot_general(a, b, (((1,), (1,)), ((), ())), preferred_element_type=F32)


def _split3(a):
    hi = a.astype(BF16)
    r = a - hi.astype(F32)
    mid = r.astype(BF16)
    lo = (r - mid.astype(F32)).astype(BF16)
    return hi, mid, lo


def _silu(v):
    return v * jax.nn.sigmoid(v)


def _layer_norm(z, g, b):
    mu = jnp.mean(z, axis=-1, keepdims=True)
    zc = z - mu
    var = jnp.mean(zc * zc, axis=-1, keepdims=True)
    return zc * lax.rsqrt(var + LN_EPS) * g + b


def _modulate(x, mod_ref):
    return x * (1.0 + mod_ref[0, 1:2, :]) + mod_ref[0, 0:1, :]


def _adaln_kernel(c_ref, w_ref, b_ref, o_ref):
    cond = _silu(c_ref[...])
    c_hi, c_lo, _ = _split3(cond)
    w_hi, w_lo, _ = _split3(w_ref[0])
    acc = _dot(c_hi, w_hi) + (_dot(c_hi, w_lo) + _dot(c_lo, w_hi))
    o_ref[0] = acc + b_ref[0]


def _adaln(c, ada_w, ada_b):
    depth, d, n = ada_w.shape
    b = c.shape[0]
    rows = -(-b // HALO) * HALO
    c_pad = jnp.zeros((rows, d), F32).at[:b].set(c)
    tn = 1024 if n % 1024 == 0 else n
    out = pl.pallas_call(
        _adaln_kernel,
        grid=(depth, n // tn),
        in_specs=[
            pl.BlockSpec((rows, d), lambda i, j: (0, 0)),
            pl.BlockSpec((1, d, tn), lambda i, j: (i, 0, j)),
            pl.BlockSpec((1, 1, tn), lambda i, j: (i, 0, j)),
        ],
        out_specs=pl.BlockSpec((1, rows, tn), lambda i, j: (i, 0, j)),
        out_shape=jax.ShapeDtypeStruct((depth, rows, n), F32),
        compiler_params=_cparams("arbitrary", "arbitrary"),
        name="adaln",
    )(c_pad, ada_w, ada_b.reshape(depth, 1, n))
    return out[:, :b].reshape(depth, b, N_SUB, 3, d)


def _ffn_kernel(x_ref, mod_ref, wg_ref, wu_ref, wd_ref, lng_ref, lnb_ref, o_ref, *, alpha, n_sub):
    rows = x_ref.shape[0] // n_sub
    for s in range(n_sub):
        rs = slice(s * rows, (s + 1) * rows)
        x = x_ref[rs, :]
        u = _modulate(x, mod_ref).astype(BF16)
        g = _dot(u, wg_ref[...])
        v = _dot(u, wu_ref[...])
        h = (_silu(g) * v).astype(BF16)
        y = _dot(h, wd_ref[...])
        z = alpha * x + (0.5 * (1.0 + mod_ref[0, 2:3, :])) * y
        o_ref[rs, :] = _layer_norm(z, lng_ref[...], lnb_ref[...])


def _row_tile(t, want):
    tm = min(want, t)
    assert t % tm == 0
    return tm


def _ffn(x, mod, wg, wu, wd, lng, lnb, *, layer, sub, seq, alpha, tm=1024, sub_rows=256):
    n, d = x.shape
    f = wg.shape[-1]
    tm = _row_tile(seq, tm)
    n_sub = max(tm // sub_rows, 1)
    per_b = seq // tm
    return pl.pallas_call(
        functools.partial(_ffn_kernel, alpha=alpha, n_sub=n_sub),
        grid=(n // tm,),
        in_specs=[
            pl.BlockSpec((tm, d), lambda i: (i, 0)),
            pl.BlockSpec((1, 3, d), lambda i: (i // per_b, 0, 0)),
            _resident((None, None, d, f), lambda i: (layer, sub, 0, 0)),
            _resident((None, None, d, f), lambda i: (layer, sub, 0, 0)),
            _resident((None, None, f, d), lambda i: (layer, sub, 0, 0)),
            _resident((1, d), lambda i: (0, 0)),
            _resident((1, d), lambda i: (0, 0)),
        ],
        out_specs=pl.BlockSpec((tm, d), lambda i: (i, 0)),
        out_shape=jax.ShapeDtypeStruct((n, d), F32),
        compiler_params=_cparams("arbitrary"),
        name="ffn_ln",
    )(x, mod, wg, wu, wd, lng.reshape(1, d), lnb.reshape(1, d))


def _proj_ln_kernel(a_ref, x_ref, mod_ref, w_ref, lng_ref, lnb_ref, o_ref, *, alpha, n_sub):
    rows = x_ref.shape[0] // n_sub
    for s in range(n_sub):
        rs = slice(s * rows, (s + 1) * rows)
        y = _dot(a_ref[rs, :], w_ref[...])
        z = alpha * x_ref[rs, :] + (1.0 + mod_ref[0, 2:3, :]) * y
        o_ref[rs, :] = _layer_norm(z, lng_ref[...], lnb_ref[...])


def _proj_ln(a, x, mod, w, lng, lnb, *, seq, alpha, tm=1024, sub_rows=256):
    n, d = x.shape
    k = a.shape[1]
    tm = _row_tile(seq, tm)
    n_sub = max(tm // sub_rows, 1)
    per_b = seq // tm
    return pl.pallas_call(
        functools.partial(_proj_ln_kernel, alpha=alpha, n_sub=n_sub),
        grid=(n // tm,),
        in_specs=[
            pl.BlockSpec((tm, k), lambda i: (i, 0)),
            pl.BlockSpec((tm, d), lambda i: (i, 0)),
            pl.BlockSpec((1, 3, d), lambda i: (i // per_b, 0, 0)),
            _resident((k, d), lambda i: (0, 0)),
            _resident((1, d), lambda i: (0, 0)),
            _resident((1, d), lambda i: (0, 0)),
        ],
        out_specs=pl.BlockSpec((tm, d), lambda i: (i, 0)),
        out_shape=jax.ShapeDtypeStruct((n, d), F32),
        compiler_params=_cparams("arbitrary"),
        name="proj_ln",
    )(a, x, mod, w, lng.reshape(1, d), lnb.reshape(1, d))


def _ssm_in_kernel(x_ref, xp_ref, xn_ref, mod_ref, w_ref, cw_ref, cb_ref, dtb_ref, alog_ref,
                   z_ref, act_ref, cs_ref, ctd_ref, wT_ref, p_ref, *, d_inner, conv_ch, per_b, n_heads):
    tm = x_ref.shape[0]
    i = pl.program_id(0)
    first = (i % per_b) == 0
    last = (i % per_b) == per_b - 1
    u_main = _modulate(x_ref[...], mod_ref)
    u_prev = jnp.where(first, 0.0, _modulate(xp_ref[...], mod_ref))
    u_next = jnp.where(last, 0.0, _modulate(xn_ref[...], mod_ref))
    u = u_main.astype(BF16)
    u_halo = jnp.concatenate([u_prev, u_main, u_next], axis=0).astype(BF16)
    rows = tm + 2 * HALO
    cblk = p_ref.shape[2]
    n_conv = conv_ch // cblk
    n_z = d_inner // cblk

    def project(k):
        c0 = d_inner + k * cblk
        p_ref[k % 2] = _dot(u_halo, w_ref[:, c0:c0 + cblk])

    def conv_act(k):
        cs_ = slice(k * cblk, (k + 1) * cblk)
        p = p_ref[k % 2]
        p_prev = pltpu.roll(p, 1, axis=0)[HALO:HALO + tm]
        p_next = pltpu.roll(p, rows - 1, axis=0)[HALO:HALO + tm]
        conv = (p_prev * cw_ref[0:1, cs_] + p[HALO:HALO + tm] * cw_ref[1:2, cs_]
                + p_next * cw_ref[2:3, cs_] + cb_ref[0:1, cs_])
        act_ref[:, cs_] = _silu(conv).astype(act_ref.dtype)

    def z_chunk(j):
        z_ref[:, j * cblk:(j + 1) * cblk] = _dot(u, w_ref[:, j * cblk:(j + 1) * cblk]).astype(z_ref.dtype)

    project(0)

    L = SSD_CHUNK
    x_dt = _dot(u, w_ref[:, d_inner + conv_ch:]) + dtb_ref[...]
    dt = jnp.maximum(x_dt, 0.0) + jnp.log1p(jnp.exp(-jnp.abs(x_dt)))
    a = dt * (-jnp.exp(alog_ref[...]))
    ri = lax.broadcasted_iota(jnp.int32, (L, L), 0)
    ci = lax.broadcasted_iota(jnp.int32, (L, L), 1)
    tri = jnp.where(ci <= ri, 1.0, 0.0).astype(BF16)
    fwd_lane = lax.broadcasted_iota(jnp.int32, (L, LANES), 1) < n_heads
    fwd_row = lax.broadcasted_iota(jnp.int32, (LANES, L), 0) < n_heads
    for k in range(tm // L):
        rs = slice(k * L, (k + 1) * L)
        a_k = a[rs]
        a_hi, a_mid, a_lo = _split3(a_k)
        cs_f = _dot(tri, a_hi) + (_dot(tri, a_mid) + _dot(tri, a_lo))
        cs_b = cs_f[L - 1:L, :] - cs_f + a_k
        cs = jnp.where(fwd_lane, cs_f, cs_b)
        csT = cs.T
        dtT = dt[rs].T
        totT = jnp.where(fwd_row, csT[:, L - 1:L], csT[:, 0:1])
        cs_ref[rs, :] = cs * LOG2E
        ctd_ref[k] = (csT - jnp.log(dtT)) * LOG2E
        wT_ref[k] = jnp.exp(totT - csT) * dtT

    for k in range(n_conv):
        if k + 1 < n_conv:
            project(k + 1)
        conv_act(k)
        for j in range(k * n_z // n_conv, (k + 1) * n_z // n_conv):
            z_chunk(j)


def _ssm_in(x, mod, w_pad, conv_w, conv_b, dt_bias_l, a_log_l, *, seq, d_inner, conv_ch, n_heads, tm=512, cblk=256):
    n, d = x.shape
    tm = _row_tile(seq, tm)
    per_b = seq // tm
    hb = tm // HALO
    n_hb = n // HALO
    L = SSD_CHUNK
    assert conv_ch % cblk == 0 and d_inner % cblk == 0 and tm % L == 0
    return pl.pallas_call(
        functools.partial(_ssm_in_kernel, d_inner=d_inner, conv_ch=conv_ch, per_b=per_b, n_heads=n_heads),
        grid=(n // tm,),
        in_specs=[
            pl.BlockSpec((tm, d), lambda i: (i, 0)),
            pl.BlockSpec((HALO, d), lambda i: (jnp.maximum(i * hb - 1, 0), 0)),
            pl.BlockSpec((HALO, d), lambda i: (jnp.minimum((i + 1) * hb, n_hb - 1), 0)),
            pl.BlockSpec((1, 3, d), lambda i: (i // per_b, 0, 0)),
            _resident((d, d_inner + conv_ch + LANES), lambda i: (0, 0)),
            _resident((3, conv_ch), lambda i: (0, 0)),
            _resident((1, conv_ch), lambda i: (0, 0)),
            _resident((1, LANES), lambda i: (0, 0)),
            _resident((1, LANES), lambda i: (0, 0)),
        ],
        out_specs=[
            pl.BlockSpec((tm, d_inner), lambda i: (i, 0)),
            pl.BlockSpec((tm, conv_ch), lambda i: (i, 0)),
            pl.BlockSpec((tm, LANES), lambda i: (i, 0)),
            pl.BlockSpec((tm // L, LANES, L), lambda i: (i, 0, 0)),
            pl.BlockSpec((tm // L, LANES, L), lambda i: (i, 0, 0)),
        ],
        out_shape=[
            jax.ShapeDtypeStruct((n, d_inner), BF16),
            jax.ShapeDtypeStruct((n, conv_ch), BF16),
            jax.ShapeDtypeStruct((n, LANES), F32),
            jax.ShapeDtypeStruct((n // L, LANES, L), F32),
            jax.ShapeDtypeStruct((n // L, LANES, L), F32),
        ],
        scratch_shapes=[pltpu.VMEM((2, tm + 2 * HALO, cblk), F32)],
        compiler_params=_cparams("arbitrary"),
        name="ssm_in",
    )(x, x, x, mod, w_pad, conv_w, conv_b.reshape(1, conv_ch), dt_bias_l, a_log_l)


def _ssd_kernel(*refs, reverse, final, n_heads):
    if final:
        act_ref, cs_ref, ctd_ref, wT_ref, z_ref, yb_ref, dsk_ref, ng_ref, out_ref, state_ref, y_ref = refs
    else:
        act_ref, cs_ref, ctd_ref, wT_ref, out_ref, state_ref = refs
    L = SSD_CHUNK
    P = SSM_HEADDIM
    G = SSM_GROUPS
    NS = SSM_STATE
    H = n_heads
    HPG = H // G
    d_inner = H * P
    assert L == 2 * P == LANES and HPG % 2 == 0

    @pl.when(pl.program_id(1) == 0)
    def _():
        state_ref[...] = jnp.zeros_like(state_ref)

    lane0 = H if reverse else 0
    cs = cs_ref[0]
    csT_dt = ctd_ref[0, 0]
    wT = wT_ref[0, 0]
    ri = lax.broadcasted_iota(jnp.int32, (L, L), 0)
    ci = lax.broadcasted_iota(jnp.int32, (L, L), 1)
    mask = (ci >= ri) if reverse else (ci <= ri)
    r_tot = 0 if reverse else L - 1
    lane = lax.broadcasted_iota(jnp.int32, (L, LANES), 1)
    lo_half = lane < P

    for g in range(G):
        b0 = d_inner + g * NS
        c0 = d_inner + G * NS + g * NS
        Bg = act_ref[0, :, b0:b0 + NS]
        Cg = act_ref[0, :, c0:c0 + NS]
        CB = _dot_nt(Cg, Bg)
        BgT = Bg.astype(F32).T
        yoff = _dot(Cg, state_ref[g].astype(BF16))
        for k in range(HPG // 2):
            h0 = g * HPG + 2 * k
            ms, bws, es = [], [], []
            for h in (h0, h0 + 1):
                hl = lane0 + h
                colb = jnp.broadcast_to(cs[:, hl:hl + 1], (L, L))
                seg = colb - csT_dt[hl:hl + 1, :]
                lm = jnp.exp2(jnp.where(mask, seg, NEG_BIG))
                ms.append((CB * lm).astype(BF16))
                bws.append((BgT * wT[hl:hl + 1, :]).astype(BF16))
                es.append(jnp.exp2(colb))
            lhs = jnp.concatenate(
                [jnp.concatenate(ms, axis=1), jnp.concatenate(bws, axis=1)], axis=0)
            xp = act_ref[0, :, h0 * P:(h0 + 2) * P]
            zero = jnp.zeros_like(xp)
            rhs = jnp.concatenate([jnp.where(lo_half, xp, zero), jnp.where(lo_half, zero, xp)], axis=0)
            res = _dot(lhs, rhs)
            e_pair = jnp.where(lo_half, es[0], es[1])
            lsl = slice(2 * k * P, (2 * k + 2) * P)
            y_pair = res[:L] + e_pair * yoff[:, lsl]
            state_ref[g, :, lsl] = state_ref[g, :, lsl] * e_pair[r_tot:r_tot + 1, :] + res[L:]
            osl = slice(h0 * P, (h0 + 2) * P)
            if final:
                y_ref[:, osl] = y_pair + yb_ref[0, :, osl].astype(F32) + xp.astype(F32) * dsk_ref[0:1, osl]
            else:
                out_ref[0, :, osl] = y_pair.astype(out_ref.dtype)

    if final:
        gw = d_inner // G
        for g in range(G):
            gs = slice(g * gw, (g + 1) * gw)
            y = y_ref[:, gs] * _silu(z_ref[0, :, gs].astype(F32))
            ms_ = jnp.mean(y * y, axis=-1, keepdims=True)
            out_ref[0, :, gs] = (y * lax.rsqrt(ms_ + LN_EPS) * ng_ref[0:1, gs]).astype(out_ref.dtype)


def _ssd(act, cs, ctd, wT, finish, *, reverse, n_heads):
    b, t, conv_ch = act.shape
    L = SSD_CHUNK
    nc = t // L
    d_inner = n_heads * SSM_HEADDIM

    def chunk(c):
        return (nc - 1 - c) if reverse else c

    def blk(w):
        return pl.BlockSpec((1, L, w), lambda i, c: (i, chunk(c), 0))

    def const(w):
        return _resident((1, w), lambda i, c: (0, 0))

    tblk = pl.BlockSpec((1, 1, LANES, L), lambda i, c: (i, chunk(c), 0, 0))
    in_specs = [blk(conv_ch), blk(LANES), tblk, tblk]
    args = [act, cs, ctd, wT]
    scratch = [pltpu.VMEM((SSM_GROUPS, SSM_STATE, d_inner // SSM_GROUPS), F32)]
    if finish is not None:
        z, y_other, d_skip_l, norm_g = finish
        in_specs += [blk(d_inner), blk(d_inner), const(d_inner), const(d_inner)]
        args += [z, y_other, d_skip_l, norm_g.reshape(1, d_inner)]
        scratch.append(pltpu.VMEM((L, d_inner), F32))
    return pl.pallas_call(
        functools.partial(_ssd_kernel, reverse=reverse, final=finish is not None, n_heads=n_heads),
        grid=(b, nc),
        in_specs=in_specs,
        out_specs=blk(d_inner),
        out_shape=jax.ShapeDtypeStruct((b, t, d_inner), BF16),
        scratch_shapes=scratch,
        compiler_params=_cparams("arbitrary", "arbitrary"),
        name="ssd_bwd" if reverse else "ssd_fwd_final",
    )(*args)


def _lane_pack(v2, n_heads):
    flat = v2.reshape(1, 2 * n_heads).astype(F32)
    return jnp.zeros((1, LANES), F32).at[:, :2 * n_heads].set(flat)


def _mamba2_bidir(x, mod, w_in, conv_w, conv_b, dt_bias, a_log, d_skip, norm_g, *, batch, seq):
    d = x.shape[1]
    n_heads = d_skip.shape[0]
    d_inner = n_heads * SSM_HEADDIM
    conv_ch = conv_w.shape[1]
    assert 2 * n_heads <= LANES
    w_pad = jnp.zeros((d, d_inner + conv_ch + LANES), BF16).at[:, :w_in.shape[1]].set(w_in.astype(BF16))
    dtb = _lane_pack(dt_bias, n_heads)
    alog = _lane_pack(a_log, n_heads)
    z, act, cs, ctd, wT = _ssm_in(x, mod, w_pad, conv_w, conv_b, dtb, alog, seq=seq, d_inner=d_inner,
                                  conv_ch=conv_ch, n_heads=n_heads)
    nc = seq // SSD_CHUNK
    z = z.reshape(batch, seq, d_inner)
    act = act.reshape(batch, seq, conv_ch)
    cs = cs.reshape(batch, seq, LANES)
    ctd = ctd.reshape(batch, nc, LANES, SSD_CHUNK)
    wT = wT.reshape(batch, nc, LANES, SSD_CHUNK)
    dsk = jnp.repeat(d_skip.astype(F32), SSM_HEADDIM).reshape(1, d_inner)
    yb = _ssd(act, cs, ctd, wT, None, reverse=True, n_heads=n_heads)
    y = _ssd(act, cs, ctd, wT, (z, yb, dsk, norm_g), reverse=False, n_heads=n_heads)
    return y.reshape(batch * seq, d_inner)


ALIBI_BLOCK = 256
N_EXT = 8
DEN_ROWS = 16


def _alibi_consts(n_heads):
    out = []
    for h in range(n_heads):
        c = 2.0 ** (-8.0 * (h + 1) / n_heads) * LOG2E
        hi = float(np.float32(c).astype(jnp.bfloat16))
        lo = float(np.float32(c - hi).astype(jnp.bfloat16))
        out.append((hi, lo))
    return out


def _qkv_kernel(x_ref, mod_ref, w_ref, qT_ref, k_ref, vT_ref, *, n_heads, dh, scale, per_b, consts):
    tm = x_ref.shape[0]
    u = _modulate(x_ref[...], mod_ref).astype(BF16)
    p = _dot(u, w_ref[...])
    hq = n_heads * LANES
    t0 = (pl.program_id(0) % per_b) * tm
    rq = lax.broadcasted_iota(jnp.int32, (LANES, tm), 0)
    pos_q = lax.broadcasted_iota(jnp.int32, (LANES, tm), 1) + t0
    q_lo = (pos_q & (ALIBI_BLOCK - 1)).astype(F32)
    q_hi = pos_q.astype(F32) - q_lo
    lk = lax.broadcasted_iota(jnp.int32, (tm, LANES), 1)
    pos_k = lax.broadcasted_iota(jnp.int32, (tm, LANES), 0) + t0
    k_lo = (pos_k & (ALIBI_BLOCK - 1)).astype(F32)
    k_hi = pos_k.astype(F32) - k_lo

    def ext_q(e0, hi, lo):
        r = rq - e0
        v = jnp.where(r < 2, hi, jnp.where(r < 4, lo, jnp.where((r & 1) == 0, -q_lo, -q_hi)))
        return jnp.where((r >= 0) & (r < N_EXT), v, 0.0)

    def ext_k(e0, hi, lo):
        l = lk - e0
        v = jnp.where(l < 4, jnp.where((l & 1) == 0, k_lo, k_hi), jnp.where(l < 6, hi, lo))
        return jnp.where((l >= 0) & (l < N_EXT), v, 0.0)

    for h in range(n_heads):
        hi, lo = consts[h]
        hs = slice(h * LANES, (h + 1) * LANES)
        qT = (p[:, hs] * (scale * LOG2E)).T
        kh = p[:, hq + h * LANES:hq + (h + 1) * LANES]
        qT_ref[0, h, 0] = jnp.where(rq < dh, qT, ext_q(dh, hi, lo)).astype(BF16)
        qT_ref[0, h, 1] = jnp.where(rq >= dh, qT, ext_q(0, hi, lo)).astype(BF16)
        k_ref[0, h, 0] = jnp.where(lk < dh, kh, ext_k(dh, hi, lo)).astype(BF16)
        k_ref[0, h, 1] = jnp.where(lk >= dh, kh, ext_k(0, hi, lo)).astype(BF16)
        vT_ref[0, h, 0:LANES] = p[:, 2 * hq + h * LANES:2 * hq + (h + 1) * LANES].T.astype(BF16)
        vT_ref[0, h, LANES:LANES + DEN_ROWS] = jnp.ones((DEN_ROWS, tm), BF16)


def _qkv(x, mod, w, *, batch, seq, n_heads, dh, scale, consts, tm=512):
    n, d = x.shape
    wn = w.shape[1]
    tm = _row_tile(seq, tm)
    per_b = seq // tm
    return pl.pallas_call(
        functools.partial(_qkv_kernel, n_heads=n_heads, dh=dh, scale=scale, per_b=per_b, consts=consts),
        grid=(n // tm,),
        in_specs=[
            pl.BlockSpec((tm, d), lambda i: (i, 0)),
            pl.BlockSpec((1, 3, d), lambda i: (i // per_b, 0, 0)),
            _resident((d, wn), lambda i: (0, 0)),
        ],
        out_specs=[
            pl.BlockSpec((1, n_heads, 2, LANES, tm), lambda i: (i // per_b, 0, 0, 0, i % per_b)),
            pl.BlockSpec((1, n_heads, 2, tm, LANES), lambda i: (i // per_b, 0, 0, i % per_b, 0)),
            pl.BlockSpec((1, n_heads, LANES + DEN_ROWS, tm), lambda i: (i // per_b, 0, 0, i % per_b)),
        ],
        out_shape=[
            jax.ShapeDtypeStruct((batch, n_heads, 2, LANES, seq), BF16),
            jax.ShapeDtypeStruct((batch, n_heads, 2, seq, LANES), BF16),
            jax.ShapeDtypeStruct((batch, n_heads, LANES + DEN_ROWS, seq), BF16),
        ],
        compiler_params=_cparams("arbitrary"),
        name="qkv",
    )(x, mod, w)


def _attn_kernel(c2_ref, lam_ref, g_ref, qp_ref, qn_ref, kc_ref, kn_ref, v_ref, o_ref,
                 sx_ref, sy_ref, mxx_ref, *, seq, dh, n_heads, n_blocks, nb, lambda_init):
    g = pl.program_id(0)
    tb = ALIBI_BLOCK
    nk = seq // tb
    steps_per_bh = nk // nb
    h_cur = (g // steps_per_bh) % n_heads
    blk_first = (g % steps_per_bh) * nb
    nxt = jnp.minimum(nb * (g + 1), n_blocks - 1)
    h_next = (nxt // nk) % n_heads
    blk_next = nxt % nk

    lam = lam_ref[...]
    lam_full = (jnp.exp(jnp.sum(lam[0:1] * lam[1:2], axis=-1, keepdims=True))
                - jnp.exp(jnp.sum(lam[2:3] * lam[3:4], axis=-1, keepdims=True)) + lambda_init)
    row = lax.broadcasted_iota(jnp.int32, (LANES, tb), 0)
    jj = lax.broadcasted_iota(jnp.int32, (tb, tb), 0)
    ii = lax.broadcasted_iota(jnp.int32, (tb, tb), 1)
    dist = (ii - jj).astype(F32)

    def scores_phase(q_maps, blk, head, k_ref, s_ref):
        q_vars = []
        for m in range(2):
            e0 = dh if m == 0 else 0
            q_left = q_maps[m]
            q_right = jnp.where((row >= e0) & (row < e0 + N_EXT), -q_left, q_left)
            q_vars.append((q_left, q_right))
        mx = [jnp.full((1, tb), NEG_BIG, F32), jnp.full((1, tb), NEG_BIG, F32)]

        def block(j):
            for m in range(2):
                q_var = jnp.where(j > blk, q_vars[m][1], q_vars[m][0])
                sT = _dot(k_ref[0, m, j * tb:(j + 1) * tb, :], q_var)
                s_ref[m, j * tb:(j + 1) * tb, :] = sT
                bm = jnp.max(sT, axis=0, keepdims=True)
                mx[m] = jnp.maximum(mx[m], jnp.where(j == blk, NEG_BIG, bm))

        def finish():
            fix = jnp.minimum(dist * c2_ref[head], 0.0)
            rows = pl.ds(pl.multiple_of(blk * tb, tb), tb)
            for m in range(2):
                sd = s_ref[m, rows, :] + fix
                s_ref[m, rows, :] = sd
                mx[m] = jnp.maximum(mx[m], jnp.max(sd, axis=0, keepdims=True))
            return mx

        return block, finish

    def values_phase(s_ref, mx, out_rows):
        acc = [jnp.zeros((LANES + DEN_ROWS, tb), F32), jnp.zeros((LANES + DEN_ROWS, tb), F32)]

        def block(j):
            for m in range(2):
                p = jnp.exp2(s_ref[m, j * tb:(j + 1) * tb, :] - mx[m])
                acc[m] = acc[m] + _dot(v_ref[0, :, j * tb:(j + 1) * tb], p.astype(BF16))

        def finish():
            outs = [a[:LANES] * (1.0 / a[LANES:LANES + 1]) for a in acc]
            oT = outs[0] - lam_full * outs[1]
            ms = jnp.mean(oT * oT, axis=0, keepdims=True)
            on = oT * lax.rsqrt(ms + LN_EPS) * g_ref[...]
            o_ref[out_rows, :] = on.T.astype(o_ref.dtype)

        return block, finish

    def run(phase_a, phase_b):
        for j in range(nk):
            phase_a[0](j)
            phase_b[0](j)
        res = phase_a[1]()
        phase_b[1]()
        return res

    def q_block(r):
        return [qp_ref[0, 0, :, r * tb:(r + 1) * tb], qp_ref[0, 1, :, r * tb:(r + 1) * tb]]

    @pl.when(g == 0)
    def _():
        blk, fin = scores_phase(q_block(0), blk_first, h_cur, kc_ref, sx_ref)
        for j in range(nk):
            blk(j)
        mx0 = fin()
        mxx_ref[0] = mx0[0]
        mxx_ref[1] = mx0[1]

    mx = [mxx_ref[0], mxx_ref[1]]
    s_bufs = (sx_ref, sy_ref)
    for r in range(nb):
        s_cur, s_nxt = s_bufs[r % 2], s_bufs[(r + 1) % 2]
        if r + 1 < nb:
            scores = scores_phase(q_block(r + 1), blk_first + r + 1, h_cur, kc_ref, s_nxt)
        else:
            scores = scores_phase([qn_ref[0, 0], qn_ref[0, 1]], blk_next, h_next, kn_ref, s_nxt)
        mx = run(scores, values_phase(s_cur, mx, slice(r * tb, (r + 1) * tb)))
    mxx_ref[0] = mx[0]
    mxx_ref[1] = mx[1]


def _diff_attention(x, mod, w_qkv, lam, subln_g, *, batch, seq, lambda_init, nb=4):
    n_heads = DA_HEADS
    dh = lam.shape[1]
    assert 2 * dh == LANES and w_qkv.shape[1] == 3 * n_heads * LANES and dh >= N_EXT
    tb = ALIBI_BLOCK
    nk = seq // tb
    nb = min(nb, nk)
    assert nb % 2 == 0 and seq % (nb * tb) == 0 and nk <= 256
    consts = _alibi_consts(n_heads)
    qT, k, vT = _qkv(x, mod, w_qkv.astype(BF16), batch=batch, seq=seq, n_heads=n_heads, dh=dh,
                     scale=dh ** -0.5, consts=consts)
    n_bh = batch * n_heads
    qT = qT.reshape(n_bh, 2, LANES, seq)
    k = k.reshape(n_bh, 2, seq, LANES)
    vT = vT.reshape(n_bh, LANES + DEN_ROWS, seq)
    spb = nk // nb
    n_blocks = n_bh * nk

    def nxt(g):
        return jnp.minimum(nb * (g + 1), n_blocks - 1)

    c2 = jnp.asarray([2.0 * (hi + lo) for hi, lo in consts], F32)
    g_col = (subln_g.astype(F32) * (1.0 - lambda_init)).reshape(LANES, 1)
    return pl.pallas_call(
        functools.partial(_attn_kernel, seq=seq, dh=dh, n_heads=n_heads, n_blocks=n_blocks, nb=nb,
                          lambda_init=lambda_init),
        grid=(n_bh * spb,),
        in_specs=[
            pl.BlockSpec(memory_space=pltpu.SMEM),
            _resident((4, dh), lambda g: (0, 0)),
            _resident((LANES, 1), lambda g: (0, 0)),
            pl.BlockSpec((1, 2, LANES, nb * tb), lambda g: (g // spb, 0, 0, g % spb)),
            pl.BlockSpec((1, 2, LANES, tb), lambda g: (nxt(g) // nk, 0, 0, nxt(g) % nk)),
            pl.BlockSpec((1, 2, seq, LANES), lambda g: (g // spb, 0, 0, 0)),
            pl.BlockSpec((1, 2, seq, LANES), lambda g: (nxt(g) // nk, 0, 0, 0)),
            pl.BlockSpec((1, LANES + DEN_ROWS, seq), lambda g: (g // spb, 0, 0)),
        ],
        out_specs=pl.BlockSpec(
            (nb * tb, LANES), lambda g: ((g // spb // n_heads) * spb + g % spb, (g // spb) % n_heads)),
        out_shape=jax.ShapeDtypeStruct((batch * seq, n_heads * LANES), BF16),
        scratch_shapes=[
            pltpu.VMEM((2, seq, tb), F32),
            pltpu.VMEM((2, seq, tb), F32),
            pltpu.VMEM((2, 1, tb), F32),
        ],
        compiler_params=_cparams("arbitrary"),
        name="diff_attn",
    )(c2, lam.astype(F32), g_col, qT, qT, k, k, vT)


def kernel(x, c, ada_w, ada_b, ln_g, ln_b, ffn_w_gate, ffn_w_up, ffn_w_down, ssm_w_in, ssm_conv_w, ssm_conv_b, ssm_dt_bias, ssm_a_log, ssm_d, ssm_norm_g, ssm_w_out, attn_w_qkv, attn_lambda, attn_subln_g, attn_w_out):
    batch, seq, d = x.shape
    depth = ada_w.shape[0]
    alpha = (2 * depth) ** 0.25
    mods = _adaln(c, ada_w, ada_b)
    xf = x.reshape(batch * seq, d)
    wg, wu, wd = ffn_w_gate.astype(BF16), ffn_w_up.astype(BF16), ffn_w_down.astype(BF16)
    for i in range(depth):
        ffn = functools.partial(_ffn, wg=wg, wu=wu, wd=wd, layer=i, seq=seq, alpha=alpha)
        xf = ffn(xf, mods[i, :, 0], lng=ln_g[i, 0], lnb=ln_b[i, 0], sub=0)
        li = i // N_MIXERS
        if i % N_MIXERS == 0:
            y = _mamba2_bidir(xf, mods[i, :, 1], ssm_w_in[li], ssm_conv_w[li], ssm_conv_b[li], ssm_dt_bias[li],
                              ssm_a_log[li], ssm_d[li], ssm_norm_g[li], batch=batch, seq=seq)
            w_out = ssm_w_out[li]
        else:
            lambda_init = 0.8 - 0.6 * math.exp(-0.3 * i)
            y = _diff_attention(xf, mods[i, :, 1], attn_w_qkv[li], attn_lambda[li], attn_subln_g[li],
                                batch=batch, seq=seq, lambda_init=lambda_init)
            w_out = attn_w_out[li]
        xf = _proj_ln(y, xf, mods[i, :, 1], w_out.astype(BF16), ln_g[i, 1], ln_b[i, 1], seq=seq, alpha=alpha)
        xf = ffn(xf, mods[i, :, 2], lng=ln_g[i, 2], lnb=ln_b[i, 2], sub=1)
    return xf.reshape(batch, seq, d)
```
